```python
import math
import jax, jax.numpy as jnp
from jax import lax
import numpy as np

D_MODEL = 1024
BATCH = 32
SEQ = 2048
DEPTH = 4
DEC_BATCH = 32
DEC_SEQ = 64
PAST_LEN = 4096

CHUNK = 64
N_LEFT_CHUNKS = 8
WIN = N_LEFT_CHUNKS * CHUNK
BAND = WIN + CHUNK
N_HEADS = 16
HEAD_DIM = D_MODEL // N_HEADS
REL_CLIP = 256
SSM_EXPAND = 2
D_INNER = SSM_EXPAND * D_MODEL
SSM_HEAD_DIM = 64
SSM_HEADS = D_INNER // SSM_HEAD_DIM
SSM_GROUPS = 8
HEADS_PER_GROUP = SSM_HEADS // SSM_GROUPS
D_STATE = 128
CONV_WIDTH = 4
CONV_CH = D_INNER + 2 * SSM_GROUPS * D_STATE
SSD_CHUNK = 64
D_FF = 4 * D_MODEL
N_ATT_LAYERS = (DEPTH + 1) // 2
N_SSM_LAYERS = DEPTH // 2
ALPHA = (2 * DEPTH) ** 0.25
BETA = (8 * DEPTH) ** -0.25
LN_EPS = 1e-5
RMS_EPS = 1e-5
NEG_INF = -1e30

kernel_name = "hybrid_chunkband_attn_mamba2_deepnorm_stream_step"


def _layer_norm(x, g, b):
    xf = x.astype(jnp.float32)
    mu = jnp.mean(xf, axis=-1, keepdims=True)
    var = jnp.mean(jnp.square(xf - mu), axis=-1, keepdims=True)
    return ((xf - mu) * lax.rsqrt(var + LN_EPS) * g + b).astype(x.dtype)


def _band_attention(q, k, v, k_past, v_past, rel_table):
    bsz, t = q.shape[0], q.shape[1]
    p = k_past.shape[1]
    n_chunks = -(-t // CHUNK)
    pad_new = n_chunks * CHUNK - t
    qp = jnp.pad(q, ((0, 0), (0, pad_new), (0, 0), (0, 0)))
    kv_pad = ((0, 0), (WIN - p, pad_new), (0, 0), (0, 0))
    kbuf = jnp.pad(jnp.concatenate([k_past.astype(k.dtype), k], axis=1), kv_pad)
    vbuf = jnp.pad(jnp.concatenate([v_past.astype(v.dtype), v], axis=1), kv_pad)
    j = jnp.arange(BAND)
    r = jnp.arange(CHUNK)
    rel = jnp.clip(WIN + r[:, None] - j[None, :], -REL_CLIP, REL_CLIP) + REL_CLIP
    bias = rel_table[:, rel].astype(jnp.float32)
    scale = HEAD_DIM ** -0.5

    def one_chunk(c):
        start = c * CHUNK
        qc = lax.dynamic_slice_in_dim(qp, start, CHUNK, axis=1)
        kc = lax.dynamic_slice_in_dim(kbuf, start, BAND, axis=1)
        vc = lax.dynamic_slice_in_dim(vbuf, start, BAND, axis=1)
        s = jnp.einsum('bqhd,bkhd->bhqk', qc, kc).astype(jnp.float32) * scale + bias
        kidx = start + j
        valid = (kidx >= WIN - p) & (kidx < WIN + t)
        s = jnp.where(valid, s, NEG_INF)
        w = jax.nn.softmax(s, axis=-1).astype(vc.dtype)
        return jnp.einsum('bhqk,bkhd->bqhd', w, vc)

    out = lax.map(one_chunk, jnp.arange(n_chunks))
    out = jnp.moveaxis(out, 0, 1).reshape(bsz, n_chunks * CHUNK, N_HEADS, HEAD_DIM)
    return out[:, :t]


def _attention_mixer(x, k_past, v_past, w_qkv, rel_table, w_o):
    bsz, t, _ = x.shape
    qkv = (x @ w_qkv).reshape(bsz, t, 3, N_HEADS, HEAD_DIM)
    q, k, v = qkv[:, :, 0], qkv[:, :, 1], qkv[:, :, 2]
    o = _band_attention(q, k, v, k_past, v_past, rel_table)
    return (o.reshape(bsz, t, D_MODEL) @ w_o).astype(x.dtype), k, v


def _ssd_scan(xs, dt, a, bm, cm, h0):
    bsz, t = xs.shape[0], xs.shape[1]
    n_chunks = -(-t // SSD_CHUNK)
    pad = n_chunks * SSD_CHUNK - t

    def to_chunks(u):
        u = jnp.pad(u, [(0, 0), (0, pad)] + [(0, 0)] * (u.ndim - 2))
        u = u.reshape((bsz, n_chunks, SSD_CHUNK) + u.shape[2:])
        return jnp.moveaxis(u, 1, 0)

    causal = jnp.tril(jnp.ones((SSD_CHUNK, SSD_CHUNK), dtype=bool))

    def step(h, inp):
        xc, dtc, bc, cc = inp
        cum = jnp.cumsum(dtc * a, axis=1)
        seg = cum[:, :, None] - cum[:, None, :]
        decay = jnp.exp(jnp.where(causal[None, :, :, None, None], seg, -jnp.inf))
        cb = jnp.einsum('btgn,bsgn->btsg', cc, bc)
        y = jnp.einsum('btsg,btsgr,bsgr,bsgrp->btgrp', cb, decay, dtc, xc)
        y = y + jnp.einsum('btgn,bgrpn->btgrp', cc, h) * jnp.exp(cum)[..., None]
        last = cum[:, -1]
        w_in = jnp.exp(last[:, None] - cum) * dtc
        h = h * jnp.exp(last)[..., None, None] + jnp.einsum('bsgn,bsgr,bsgrp->bgrpn', bc, w_in, xc)
        return h, y

    h, ys = lax.scan(step, h0, (to_chunks(xs), to_chunks(dt), to_chunks(bm), to_chunks(cm)))
    y = jnp.moveaxis(ys, 0, 1).reshape((bsz, n_chunks * SSD_CHUNK) + xs.shape[2:])[:, :t]
    return y, h


def _ssm_mixer(x, conv_state, ssm_state, w_in, conv_w, conv_b, dt_bias, a_log, d_skip, norm_w, w_out):
    bsz, t, _ = x.shape
    g, r, p, n = SSM_GROUPS, HEADS_PER_GROUP, SSM_HEAD_DIM, D_STATE
    proj = x @ w_in
    z = proj[..., :D_INNER]
    xbc = proj[..., D_INNER:D_INNER + CONV_CH]
    dt_raw = proj[..., D_INNER + CONV_CH:]
    xpad = jnp.concatenate([conv_state.astype(xbc.dtype), xbc], axis=1)
    conv = conv_b
    for kk in range(CONV_WIDTH):
        conv = conv + xpad[:, kk:kk + t] * conv_w[kk]
    new_conv = xpad[:, -(CONV_WIDTH - 1):]
    xbc = jax.nn.silu(conv)
    xs = xbc[..., :D_INNER].reshape(bsz, t, g, r, p)
    bm = xbc[..., D_INNER:D_INNER + g * n].reshape(bsz, t, g, n)
    cm = xbc[..., D_INNER + g * n:].reshape(bsz, t, g, n)
    dt = jax.nn.softplus((dt_raw + dt_bias).astype(jnp.float32)).reshape(bsz, t, g, r)
    a = -jnp.exp(a_log.astype(jnp.float32)).reshape(g, r)
    h0 = ssm_state.reshape(bsz, g, r, p, n).astype(jnp.float32)
    y, h = _ssd_scan(xs, dt, a, bm, cm, h0)
    y = y + d_skip.reshape(g, r)[..., None] * xs
    y = y.reshape(bsz, t, D_INNER) * jax.nn.silu(z.astype(jnp.float32))
    yg = y.reshape(bsz, t, g, D_INNER // g)
    yg = yg * lax.rsqrt(jnp.mean(jnp.square(yg), axis=-1, keepdims=True) + RMS_EPS)
    y = (yg.reshape(bsz, t, D_INNER) * norm_w).astype(x.dtype)
    return (y @ w_out).astype(x.dtype), new_conv, h.reshape(bsz, SSM_HEADS, SSM_HEAD_DIM, D_STATE)


def _mlp(x, w_up, w_down):
    return (jnp.square(jax.nn.relu(x @ w_up)) @ w_down).astype(x.dtype)


def setup_inputs(seed: int = 0) -> dict:
    key = jax.random.key(seed)
    ks = jax.random.split(key, 24)
    att_cache = min(WIN, PAST_LEN)

    def nrm(k, shape, s):
        return jax.random.normal(k, shape, jnp.float32) * s

    dt0 = jnp.exp(jax.random.uniform(ks[12], (N_SSM_LAYERS, SSM_HEADS), jnp.float32,
                                     math.log(1e-3), math.log(1e-1)))
    return {
        'x_prompt': nrm(ks[0], (BATCH, SEQ, D_MODEL), 1.0),
        'x_sample': nrm(ks[1], (DEC_BATCH, DEC_SEQ, D_MODEL), 1.0),
        'cache_k': nrm(ks[2], (N_ATT_LAYERS, DEC_BATCH, att_cache, N_HEADS, HEAD_DIM), 1.0),
        'cache_v': nrm(ks[3], (N_ATT_LAYERS, DEC_BATCH, att_cache, N_HEADS, HEAD_DIM), 1.0),
        'state_ssm': nrm(ks[4], (N_SSM_LAYERS, DEC_BATCH, SSM_HEADS, SSM_HEAD_DIM, D_STATE), 0.1),
        'state_conv': nrm(ks[5], (N_SSM_LAYERS, DEC_BATCH, CONV_WIDTH - 1, CONV_CH), 1.0),
        'w_qkv': nrm(ks[6], (N_ATT_LAYERS, D_MODEL, 3 * D_MODEL), D_MODEL ** -0.5),
        'rel_bias': nrm(ks[7], (N_ATT_LAYERS, N_HEADS, 2 * REL_CLIP + 1), 0.5),
        'w_attn_out': nrm(ks[8], (N_ATT_LAYERS, D_MODEL, D_MODEL), BETA * D_MODEL ** -0.5),
        'w_ssm_in': nrm(ks[9], (N_SSM_LAYERS, D_MODEL, D_INNER + CONV_CH + SSM_HEADS), D_MODEL ** -0.5),
        'conv_w': nrm(ks[10], (N_SSM_LAYERS, CONV_WIDTH, CONV_CH), CONV_WIDTH ** -0.5),
        'conv_b': nrm(ks[11], (N_SSM_LAYERS, CONV_CH), 0.02),
        'dt_bias': dt0 + jnp.log(-jnp.expm1(-dt0)),
        'a_log': jnp.log(jax.random.uniform(ks[13], (N_SSM_LAYERS, SSM_HEADS), jnp.float32, 1.0, 16.0)),
        'd_skip': 1.0 + nrm(ks[14], (N_SSM_LAYERS, SSM_HEADS), 0.1),
        'ssm_norm_w': 1.0 + nrm(ks[15], (N_SSM_LAYERS, D_INNER), 0.05),
        'w_ssm_out': nrm(ks[16], (N_SSM_LAYERS, D_INNER, D_MODEL), BETA * D_INNER ** -0.5),
        'ln_mix_g': 1.0 + nrm(ks[17], (DEPTH, D_MODEL), 0.05),
        'ln_mix_b': nrm(ks[18], (DEPTH, D_MODEL), 0.02),
        'w_ff_up': nrm(ks[19], (DEPTH, D_MODEL, D_FF), D_MODEL ** -0.5),
        'w_ff_down': nrm(ks[20], (DEPTH, D_FF, D_MODEL), BETA * D_FF ** -0.5),
        'ln_ff_g': 1.0 + nrm(ks[21], (DEPTH, D_MODEL), 0.05),
        'ln_ff_b': nrm(ks[22], (DEPTH, D_MODEL), 0.02),
    }


def reference(x_prompt, x_sample, cache_k, cache_v, state_ssm, state_conv,
              w_qkv, rel_bias, w_attn_out, w_ssm_in, conv_w, conv_b, dt_bias, a_log,
              d_skip, ssm_norm_w, w_ssm_out, ln_mix_g, ln_mix_b, w_ff_up, w_ff_down,
              ln_ff_g, ln_ff_b):
    bp = x_prompt.shape[0]
    keep = min(WIN, x_prompt.shape[1])
    xp, xs = x_prompt, x_sample
    kp_l, vp_l, ks_l, vs_l = [], [], [], []
    hp_l, cp_l, hs_l, cs_l = [], [], [], []
    for i in range(DEPTH):
        if i % 2 == 0:
            li = i // 2
            empty = jnp.zeros((bp, 0, N_HEADS, HEAD_DIM), xp.dtype)
            mp, kp, vp = _attention_mixer(xp, empty, empty, w_qkv[li], rel_bias[li], w_attn_out[li])
            ms, ksm, vsm = _attention_mixer(xs, cache_k[li], cache_v[li], w_qkv[li], rel_bias[li], w_attn_out[li])
            kp_l.append(kp[:, -keep:]); vp_l.append(vp[:, -keep:])
            ks_l.append(ksm); vs_l.append(vsm)
        else:
            li = i // 2
            args = (w_ssm_in[li], conv_w[li], conv_b[li], dt_bias[li], a_log[li], d_skip[li],
                    ssm_norm_w[li], w_ssm_out[li])
            conv0 = jnp.zeros((bp, CONV_WIDTH - 1, CONV_CH), xp.dtype)
            h00 = jnp.zeros((bp, SSM_HEADS, SSM_HEAD_DIM, D_STATE), jnp.float32)
            mp, cp, hp = _ssm_mixer(xp, conv0, h00, *args)
            ms, csm, hsm = _ssm_mixer(xs, state_conv[li], state_ssm[li], *args)
            hp_l.append(hp); cp_l.append(cp)
            hs_l.append(hsm); cs_l.append(csm)
        xp = _layer_norm(ALPHA * xp + mp, ln_mix_g[i], ln_mix_b[i])
        xs = _layer_norm(ALPHA * xs + ms, ln_mix_g[i], ln_mix_b[i])
        xp = _layer_norm(ALPHA * xp + _mlp(xp, w_ff_up[i], w_ff_down[i]), ln_ff_g[i], ln_ff_b[i])
        xs = _layer_norm(ALPHA * xs + _mlp(xs, w_ff_up[i], w_ff_down[i]), ln_ff_g[i], ln_ff_b[i])
    k_prompt = jnp.stack(kp_l)
    v_prompt = jnp.stack(vp_l)
    ssm_prompt = jnp.stack(hp_l)
    conv_prompt = jnp.stack(cp_l)
    k_sample = jnp.stack(ks_l)
    v_sample = jnp.stack(vs_l)
    ssm_sample = jnp.stack(hs_l)
    conv_sample = jnp.stack(cs_l)
    return (xp, xs, k_prompt, v_prompt, ssm_prompt, conv_prompt,
            k_sample, v_sample, ssm_sample, conv_sample)
```

```python
import functools

import jax
import jax.numpy as jnp
from jax import lax
from jax.experimental import pallas as pl
from jax.experimental.pallas import tpu as pltpu

F32 = jnp.float32
BF16 = jnp.bfloat16

DEPTH = 4
CHUNK = 64
WIN = 8 * CHUNK
N_HEADS = 16
HEAD_DIM = 64
REL_CLIP = 256
SSM_HEAD_DIM = 64
SSM_GROUPS = 8
D_STATE = 128
CONV_WIDTH = 4
ALPHA = (2 * DEPTH) ** 0.25
LN_EPS = 1e-5
RMS_EPS = 1e-5
NEG_INF = -1e30

LANES = 128
SUBLANES = 8
VMEM_BYTES_V7X = 64 * 1024 * 1024
VMEM_LIMIT_BYTES = VMEM_BYTES_V7X - 8 * 1024 * 1024

ROW_TILE = 512
ATTN_Q_TILE = 128
CONV_T_TILE = 256


def _params(*semantics):
    return pltpu.CompilerParams(dimension_semantics=semantics, vmem_limit_bytes=VMEM_LIMIT_BYTES)


def _layer_norm(y, g, b):
    mu = jnp.mean(y, axis=-1, keepdims=True)
    d = y - mu
    var = jnp.mean(d * d, axis=-1, keepdims=True)
    return d * lax.rsqrt(var + LN_EPS) * g + b


def _softplus(x):
    return jnp.maximum(x, 0.0) + jnp.log(1.0 + jnp.exp(-jnp.abs(x)))


def _silu(x):
    return x / (1.0 + jnp.exp(-x))


def _dense_kernel(x_ref, w_ref, o_ref, xb_ref):
    @pl.when(pl.program_id(1) == 0)
    def _():
        xb_ref[...] = x_ref[...].astype(BF16)

    acc = jnp.dot(xb_ref[...], w_ref[...], preferred_element_type=F32)
    o_ref[...] = acc.astype(o_ref.dtype).reshape(o_ref.shape)


def _dense(x, w, *, out_dtype, tn, split_out=False, row_blocks=None):
    rows, k = x.shape
    n = w.shape[1]
    tm = min(ROW_TILE, rows)
    assert rows % tm == 0 and n % tn == 0
    n_rb, rb_fn = row_blocks if row_blocks is not None else (rows // tm, lambda i: i)
    out_rows = n_rb * tm
    if split_out:
        out_shape = jax.ShapeDtypeStruct((n // tn, out_rows, tn), out_dtype)
        out_spec = pl.BlockSpec((1, tm, tn), lambda i, j: (j, i, 0))
    else:
        out_shape = jax.ShapeDtypeStruct((out_rows, n), out_dtype)
        out_spec = pl.BlockSpec((tm, tn), lambda i, j: (i, j))
    return pl.pallas_call(
        _dense_kernel,
        grid=(n_rb, n // tn),
        in_specs=[pl.BlockSpec((tm, k), lambda i, j: (rb_fn(i), 0)),
                  pl.BlockSpec((k, tn), lambda i, j: (0, j))],
        out_specs=out_spec,
        out_shape=out_shape,
        scratch_shapes=[pltpu.VMEM((tm, k), BF16)],
        compiler_params=_params("parallel", "arbitrary"),
        name="dense",
    )(x, w)


def _dense_ln_kernel(a_ref, w_ref, r_ref, g_ref, b_ref, o_ref):
    acc = jnp.dot(a_ref[...], w_ref[...], preferred_element_type=F32)
    o_ref[...] = _layer_norm(ALPHA * r_ref[...] + acc, g_ref[...], b_ref[...])


def _dense_ln(a, w, resid, g, b):
    rows, k = a.shape
    d = w.shape[1]
    tm = min(ROW_TILE, rows)
    assert rows % tm == 0
    return pl.pallas_call(
        _dense_ln_kernel,
        grid=(rows // tm,),
        in_specs=[pl.BlockSpec((tm, k), lambda i: (i, 0)),
                  pl.BlockSpec((k, d), lambda i: (0, 0)),
                  pl.BlockSpec((tm, d), lambda i: (i, 0)),
                  pl.BlockSpec((1, d), lambda i: (0, 0)),
                  pl.BlockSpec((1, d), lambda i: (0, 0))],
        out_specs=pl.BlockSpec((tm, d), lambda i: (i, 0)),
        out_shape=jax.ShapeDtypeStruct((rows, d), F32),
        compiler_params=_params("parallel"),
        name="dense_ln",
    )(a, w, resid, g.reshape(1, d), b.reshape(1, d))


def _mlp_kernel(x_ref, wu_ref, wd_ref, g_ref, b_ref, o_ref, xb_ref, acc_ref):
    f = pl.program_id(1)

    @pl.when(f == 0)
    def _():
        xb_ref[...] = x_ref[...].astype(BF16)

    h = jnp.dot(xb_ref[...], wu_ref[...], preferred_element_type=F32)
    h = jnp.square(jnp.maximum(h, 0.0)).astype(BF16)
    part = jnp.dot(h, wd_ref[...], preferred_element_type=F32)

    @pl.when(f == 0)
    def _():
        acc_ref[...] = part

    @pl.when(f > 0)
    def _():
        acc_ref[...] += part

    @pl.when(f == pl.num_programs(1) - 1)
    def _():
        o_ref[...] = _layer_norm(ALPHA * x_ref[...] + acc_ref[...], g_ref[...], b_ref[...])


def _mlp(x, w_up, w_down, g, b, *, tf=1024):
    rows, d = x.shape
    d_ff = w_up.shape[1]
    tm = min(ROW_TILE, rows)
    assert rows % tm == 0 and d_ff % tf == 0
    return pl.pallas_call(
        _mlp_kernel,
        grid=(rows // tm, d_ff // tf),
        in_specs=[pl.BlockSpec((tm, d), lambda i, f: (i, 0)),
                  pl.BlockSpec((d, tf), lambda i, f: (0, f)),
                  pl.BlockSpec((tf, d), lambda i, f: (f, 0)),
                  pl.BlockSpec((1, d), lambda i, f: (0, 0)),
                  pl.BlockSpec((1, d), lambda i, f: (0, 0))],
        out_specs=pl.BlockSpec((tm, d), lambda i, f: (i, 0)),
        out_shape=jax.ShapeDtypeStruct((rows, d), F32),
        scratch_shapes=[pltpu.VMEM((tm, d), BF16), pltpu.VMEM((tm, d), F32)],
        compiler_params=_params("parallel", "arbitrary"),
        name="mlp",
    )(x, w_up, w_down, g.reshape(1, d), b.reshape(1, d))


def _attn_kernel(*refs, tq, n_past, has_past):
    if has_past:
        q_ref, k_ref, v_ref, kp_ref, vp_ref, bias_ref, o_ref, kpad, vpad = refs
        kpad[0:WIN, :] = kp_ref[0]
        vpad[0:WIN, :] = vp_ref[0]
    else:
        q_ref, k_ref, v_ref, bias_ref, o_ref, kpad, vpad = refs
        kpad[0:WIN, :] = jnp.zeros((WIN, LANES), BF16)
        vpad[0:WIN, :] = jnp.zeros((WIN, LANES), BF16)
    t = q_ref.shape[1]
    band = WIN + tq
    kpad[WIN:WIN + t, :] = k_ref[0]
    vpad[WIN:WIN + t, :] = v_ref[0]

    first = lax.broadcasted_iota(jnp.int32, (tq, LANES), 1) < HEAD_DIM
    head_masks = (first.astype(BF16), jnp.logical_not(first).astype(BF16))
    col = lax.broadcasted_iota(jnp.int32, (tq, band), 1)
    scale = HEAD_DIM ** -0.5

    def tile(qs, carry):
        r0 = pl.multiple_of(qs * tq, tq)
        q2 = q_ref[0, pl.ds(r0, tq), :] * scale
        kw = kpad[pl.ds(r0, band), :]
        vw = vpad[pl.ds(r0, band), :]
        valid = col >= (WIN - n_past) - r0
        outs = []
        for half in range(2):
            qh = q2 * head_masks[half]
            s = lax.dot_general(qh, kw, (((1,), (1,)), ((), ())), preferred_element_type=F32)
            s = jnp.where(valid, s + bias_ref[half], NEG_INF)
            m = jnp.max(s, axis=-1, keepdims=True)
            p = jnp.exp(s - m)
            denom = jnp.sum(p, axis=-1, keepdims=True)
            o = jnp.dot(p.astype(BF16), vw, preferred_element_type=F32)
            outs.append(o / denom)
        o_ref[0, pl.ds(r0, tq), :] = jnp.where(first, outs[0], outs[1]).astype(o_ref.dtype)
        return carry

    lax.fori_loop(0, t // tq, tile, 0)


def _attn_bias(rel_table, tq):
    r = jnp.arange(tq)[:, None]
    j = jnp.arange(WIN + tq)[None, :]
    rel = jnp.clip(WIN + r - j, -REL_CLIP, REL_CLIP) + REL_CLIP
    q_chunk = r // CHUNK
    k_chunk = (j - WIN) // CHUNK
    in_band = (k_chunk >= q_chunk - WIN // CHUNK) & (k_chunk <= q_chunk)
    bias = rel_table.astype(F32)[:, rel]
    return jnp.where(in_band[None], bias, NEG_INF)


def _attention(qkv, rel_table, past=None):
    bsz, t, d3 = qkv.shape
    d = d3 // 3
    n_pairs = d // LANES
    tq = min(ATTN_Q_TILE, t)
    assert t % tq == 0 and tq % CHUNK == 0
    bias = _attn_bias(rel_table, tq)
    has_past = past is not None
    new_spec = lambda off: pl.BlockSpec((1, t, LANES), lambda hp, b: (b, 0, off * n_pairs + hp))
    in_specs = [new_spec(0), new_spec(1), new_spec(2)]
    args = [qkv, qkv, qkv]
    if has_past:
        past_spec = pl.BlockSpec((1, WIN, LANES), lambda hp, b: (b, 0, hp))
        in_specs += [past_spec, past_spec]
        args += list(past)
    in_specs.append(pl.BlockSpec((2, tq, WIN + tq), lambda hp, b: (hp, 0, 0)))
    args.append(bias)
    return pl.pallas_call(
        functools.partial(_attn_kernel, tq=tq, n_past=WIN if has_past else 0, has_past=has_past),
        grid=(n_pairs, bsz),
        in_specs=in_specs,
        out_specs=pl.BlockSpec((1, t, LANES), lambda hp, b: (b, 0, hp)),
        out_shape=jax.ShapeDtypeStruct((bsz, t, d), BF16),
        scratch_shapes=[pltpu.VMEM((WIN + t, LANES), BF16), pltpu.VMEM((WIN + t, LANES), BF16)],
        compiler_params=_params("parallel", "parallel"),
        name="band_attention",
    )(*args)


def _conv_kernel(x_ref, halo_ref, st_ref, w_ref, b_ref, o_ref, ext_ref):
    tt = x_ref.shape[1]
    ext_ref[0:SUBLANES, :] = jnp.where(pl.program_id(1) == 0, st_ref[0], halo_ref[0])
    ext_ref[SUBLANES:SUBLANES + tt, :] = x_ref[0]
    acc = jnp.broadcast_to(b_ref[...], (tt, x_ref.shape[2]))
    for kk in range(CONV_WIDTH):
        acc = acc + ext_ref[pl.ds(SUBLANES - (CONV_WIDTH - 1) + kk, tt), :] * w_ref[kk:kk + 1, :]
    o_ref[0] = _silu(acc).astype(o_ref.dtype)


def _conv_silu(xbc, conv_state, conv_w, conv_b):
    bsz, t, ch = xbc.shape
    tt = min(CONV_T_TILE, t)
    assert t % tt == 0 and tt % SUBLANES == 0
    state8 = jnp.pad(conv_state, ((0, 0), (SUBLANES - (CONV_WIDTH - 1), 0), (0, 0)))
    halo_blocks = tt // SUBLANES
    return pl.pallas_call(
        _conv_kernel,
        grid=(bsz, t // tt),
        in_specs=[pl.BlockSpec((1, tt, ch), lambda b, i: (b, i, 0)),
                  pl.BlockSpec((1, SUBLANES, ch), lambda b, i: (b, jnp.maximum(i * halo_blocks - 1, 0), 0)),
                  pl.BlockSpec((1, SUBLANES, ch), lambda b, i: (b, 0, 0)),
                  pl.BlockSpec((CONV_WIDTH, ch), lambda b, i: (0, 0)),
                  pl.BlockSpec((1, ch), lambda b, i: (0, 0))],
        out_specs=pl.BlockSpec((1, tt, ch), lambda b, i: (b, i, 0)),
        out_shape=jax.ShapeDtypeStruct((bsz, t, ch), BF16),
        scratch_shapes=[pltpu.VMEM((SUBLANES + tt, ch), F32)],
        compiler_params=_params("parallel", "parallel"),
        name="conv_silu",
    )(xbc, xbc, state8, conv_w, conv_b.reshape(1, ch))


def _dot_f32(a, b):
    return jnp.dot(a, b, preferred_element_type=F32, precision=lax.Precision.HIGHEST)


def _ssd_kernel(x_ref, bm_ref, cm_ref, z_ref, dtc_ref, dtr_ref, pc_ref, pr_ref, dsk_ref, nw_ref,
                h0_ref, y_ref, hout_ref, h_ref):
    c = pl.program_id(1)
    seq = x_ref.shape[1]
    n_pairs = x_ref.shape[2] // LANES
    pairs_per_group = n_pairs // SSM_GROUPS

    @pl.when(c == 0)
    def _():
        h_ref[...] = h0_ref[0]

    ri = lax.broadcasted_iota(jnp.int32, (seq, seq), 0)
    ci = lax.broadcasted_iota(jnp.int32, (seq, seq), 1)
    tri = (ri >= ci).astype(F32)

    dt_c = _softplus(dtc_ref[0] + pc_ref[0:1, :])
    a_c = -jnp.exp(pc_ref[1:2, :])
    cum_c = _dot_f32(tri, dt_c * a_c)

    pi = lax.broadcasted_iota(jnp.int32, (LANES, LANES), 0)
    qi = lax.broadcasted_iota(jnp.int32, (LANES, LANES), 1)
    same_head = (pi < seq) == (qi < seq)
    tri_bd = (same_head & (pi <= qi)).astype(F32)
    ones_bd = same_head.astype(F32)
    dt_r = _softplus(dtr_ref[0, 0] + pr_ref[0])
    dta_r = dt_r * (-jnp.exp(pr_ref[1]))
    cum_r = _dot_f32(dta_r, tri_bd)
    last_r = _dot_f32(dta_r, ones_bd)
    w_in_r = jnp.exp(last_r - cum_r) * dt_r

    lane = lax.broadcasted_iota(jnp.int32, (seq, LANES), 1)
    first = lane < SSM_HEAD_DIM
    causal2 = jnp.where(first, lane, lane - seq) <= lax.broadcasted_iota(jnp.int32, (seq, LANES), 0)
    first_rows = lax.broadcasted_iota(jnp.int32, (LANES, LANES), 0) < SSM_HEAD_DIM
    last_c = cum_c[seq - 1:seq, :]

    for g in range(SSM_GROUPS):
        bg = bm_ref[0, :, g * D_STATE:(g + 1) * D_STATE]
        cg = cm_ref[0, :, g * D_STATE:(g + 1) * D_STATE]
        bg2 = jnp.concatenate([bg, bg], axis=0)
        cb2 = lax.dot_general(cg, bg2, (((1,), (1,)), ((), ())), preferred_element_type=F32)
        ys = []
        for jj in range(pairs_per_group):
            j = g * pairs_per_group + jj
            h0, h1 = 2 * j, 2 * j + 1
            lo = j * LANES
            xp = x_ref[0, :, lo:lo + LANES]
            xpf = xp.astype(F32)
            col_cum = jnp.where(first, cum_c[:, h0:h0 + 1], cum_c[:, h1:h1 + 1])
            seg = col_cum - cum_r[j:j + 1, :]
            decay = jnp.exp(jnp.where(causal2, seg, -jnp.inf))
            w2 = (cb2 * decay * dt_r[j:j + 1, :]).astype(BF16)
            x_bd = jnp.concatenate([jnp.where(first, xpf, 0.0), jnp.where(first, 0.0, xpf)], axis=0)
            y2 = jnp.dot(w2, x_bd.astype(BF16), preferred_element_type=F32)
            hp = h_ref[lo:lo + LANES, :]
            y_state = lax.dot_general(cg, hp.astype(BF16), (((1,), (1,)), ((), ())),
                                      preferred_element_type=F32)
            y2 = y2 + jnp.exp(col_cum) * y_state + dsk_ref[:, lo:lo + LANES] * xpf
            xw_t = (x_bd.T * w_in_r[j:j + 1, :]).astype(BF16)
            upd = jnp.dot(xw_t, bg2, preferred_element_type=F32)
            e_last = jnp.exp(jnp.where(first_rows, last_c[:, h0:h0 + 1], last_c[:, h1:h1 + 1]))
            h_ref[lo:lo + LANES, :] = hp * e_last + upd
            ys.append(y2)
        yg = jnp.concatenate(ys, axis=1)
        glo = g * pairs_per_group * LANES
        ghi = glo + pairs_per_group * LANES
        yg = yg * _silu(z_ref[0, :, glo:ghi].astype(F32))
        ms = jnp.mean(yg * yg, axis=-1, keepdims=True)
        y_ref[0, :, glo:ghi] = (yg * lax.rsqrt(ms + RMS_EPS) * nw_ref[:, glo:ghi]).astype(y_ref.dtype)

    @pl.when(c == pl.num_programs(1) - 1)
    def _():
        hout_ref[0] = h_ref[...]


def _ssd(xa, z, dt_raw, h0, dt_bias, a_log, d_skip, norm_w):
    bsz, t, _ = xa.shape
    d_inner = z.shape[2]
    n_heads = d_inner // SSM_HEAD_DIM
    n_pairs = n_heads // 2
    gn = SSM_GROUPS * D_STATE
    assert t % CHUNK == 0 and d_inner % gn == 0
    nc = t // CHUNK
    dt_rows = dt_raw[:, :, :n_heads].reshape(bsz, nc, CHUNK, n_pairs, 2)
    dt_rows = dt_rows.transpose(0, 1, 3, 4, 2).reshape(bsz, nc, n_pairs, 2 * CHUNK)
    pad = LANES - n_heads
    p_cols = jnp.stack([jnp.pad(dt_bias, (0, pad)), jnp.pad(a_log, (0, pad))])
    p_rows = jnp.stack([jnp.repeat(dt_bias, CHUNK).reshape(n_pairs, 2 * CHUNK),
                        jnp.repeat(a_log, CHUNK).reshape(n_pairs, 2 * CHUNK)])
    d_skip_row = jnp.repeat(d_skip, SSM_HEAD_DIM).reshape(1, d_inner)
    x_blocks = d_inner // gn
    y, h_out = pl.pallas_call(
        _ssd_kernel,
        grid=(bsz, nc),
        in_specs=[pl.BlockSpec((1, CHUNK, d_inner), lambda b, c: (b, c, 0)),
                  pl.BlockSpec((1, CHUNK, gn), lambda b, c: (b, c, x_blocks)),
                  pl.BlockSpec((1, CHUNK, gn), lambda b, c: (b, c, x_blocks + 1)),
                  pl.BlockSpec((1, CHUNK, d_inner), lambda b, c: (b, c, 0)),
                  pl.BlockSpec((1, CHUNK, LANES), lambda b, c: (b, c, 0)),
                  pl.BlockSpec((1, 1, n_pairs, LANES), lambda b, c: (b, c, 0, 0)),
                  pl.BlockSpec((2, LANES), lambda b, c: (0, 0)),
                  pl.BlockSpec((2, n_pairs, LANES), lambda b, c: (0, 0, 0)),
                  pl.BlockSpec((1, d_inner), lambda b, c: (0, 0)),
                  pl.BlockSpec((1, d_inner), lambda b, c: (0, 0)),
                  pl.BlockSpec((1, d_inner, D_STATE), lambda b, c: (b, 0, 0))],
        out_specs=[pl.BlockSpec((1, CHUNK, d_inner), lambda b, c: (b, c, 0)),
                   pl.BlockSpec((1, d_inner, D_STATE), lambda b, c: (b, 0, 0))],
        out_shape=[jax.ShapeDtypeStruct((bsz, t, d_inner), BF16),
                   jax.ShapeDtypeStruct((bsz, d_inner, D_STATE), F32)],
        scratch_shapes=[pltpu.VMEM((d_inner, D_STATE), F32)],
        compiler_params=_params("parallel", "arbitrary"),
        name="ssd",
    )(xa, xa, xa, z, dt_raw, dt_rows, p_cols, p_rows, d_skip_row, norm_w.reshape(1, d_inner),
      h0.reshape(bsz, d_inner, D_STATE))
    return y, h_out.reshape(bsz, n_heads, SSM_HEAD_DIM, D_STATE)


def _attention_layer(x, bsz, w_qkv, rel_table, w_o, g, b, past):
    rows, d = x.shape
    t = rows // bsz
    qkv = _dense(x, w_qkv, out_dtype=BF16, tn=d)
    keep = min(WIN, t)
    kept = None
    if keep < t:
        tm = min(ROW_TILE, rows)
        assert t % tm == 0 and keep % tm == 0
        blocks_per_seq, blocks_kept = t // tm, keep // tm
        kept = (bsz * blocks_kept,
                lambda i: (i // blocks_kept) * blocks_per_seq + blocks_per_seq - blocks_kept + i % blocks_kept)
    kv = _dense(x, w_qkv[:, d:], out_dtype=F32, tn=d, split_out=True, row_blocks=kept)
    k_new = kv[0].reshape(bsz, keep, N_HEADS, HEAD_DIM)
    v_new = kv[1].reshape(bsz, keep, N_HEADS, HEAD_DIM)
    if past is not None:
        past = tuple(p.reshape(bsz, WIN, d).astype(BF16) for p in past)
    o = _attention(qkv.reshape(bsz, t, 3 * d), rel_table, past)
    return _dense_ln(o.reshape(rows, d), w_o, x, g, b), k_new, v_new


def _ssm_layer(x, bsz, conv_state, ssm_state, w_z, w_xbc, w_dt, conv_w, conv_b, dt_bias, a_log,
               d_skip, norm_w, w_out, g, b):
    rows, d = x.shape
    t = rows // bsz
    d_inner = w_z.shape[1]
    z = _dense(x, w_z, out_dtype=BF16, tn=1024)
    xbc = _dense(x, w_xbc, out_dtype=F32, tn=1024).reshape(bsz, t, -1)
    dt_raw = _dense(x, w_dt, out_dtype=F32, tn=LANES).reshape(bsz, t, LANES)
    assert t >= CONV_WIDTH - 1
    new_conv = xbc[:, -(CONV_WIDTH - 1):]
    xa = _conv_silu(xbc, conv_state, conv_w, conv_b)
    y, h_new = _ssd(xa, z.reshape(bsz, t, d_inner), dt_raw, ssm_state, dt_bias, a_log, d_skip, norm_w)
    return _dense_ln(y.reshape(rows, d_inner), w_out, x, g, b), new_conv, h_new


def kernel(x_prompt, x_sample, cache_k, cache_v, state_ssm, state_conv, w_qkv, rel_bias, w_attn_out,
           w_ssm_in, conv_w, conv_b, dt_bias, a_log, d_skip, ssm_norm_w, w_ssm_out, ln_mix_g, ln_mix_b,
           w_ff_up, w_ff_down, ln_ff_g, ln_ff_b):
    bp, tp, d = x_prompt.shape
    bs, ts, _ = x_sample.shape
    n_ssm_heads = dt_bias.shape[1]
    d_inner = ssm_norm_w.shape[1]
    conv_ch = conv_w.shape[2]
    xp = x_prompt.reshape(bp * tp, d)
    xs = x_sample.reshape(bs * ts, d)
    k_p, v_p, k_s, v_s, h_p, c_p, h_s, c_s = ([] for _ in range(8))
    for i in range(DEPTH):
        li = i // 2
        g, b = ln_mix_g[i], ln_mix_b[i]
        if i % 2 == 0:
            wq, wo = w_qkv[li].astype(BF16), w_attn_out[li].astype(BF16)
            xp, k, v = _attention_layer(xp, bp, wq, rel_bias[li], wo, g, b, None)
            k_p.append(k)
            v_p.append(v)
            xs, k, v = _attention_layer(xs, bs, wq, rel_bias[li], wo, g, b, (cache_k[li], cache_v[li]))
            k_s.append(k)
            v_s.append(v)
        else:
            w_in = w_ssm_in[li].astype(BF16)
            w_z = w_in[:, :d_inner]
            w_xbc = w_in[:, d_inner:d_inner + conv_ch]
            w_dt = jnp.pad(w_in[:, d_inner + conv_ch:], ((0, 0), (0, LANES - n_ssm_heads)))
            args = (w_z, w_xbc, w_dt, conv_w[li], conv_b[li], dt_bias[li], a_log[li], d_skip[li],
                    ssm_norm_w[li], w_ssm_out[li].astype(BF16), g, b)
            conv0 = jnp.zeros((bp, CONV_WIDTH - 1, conv_ch), F32)
            h00 = jnp.zeros((bp, n_ssm_heads, SSM_HEAD_DIM, D_STATE), F32)
            xp, cv, h = _ssm_layer(xp, bp, conv0, h00, *args)
            c_p.append(cv)
            h_p.append(h)
            xs, cv, h = _ssm_layer(xs, bs, state_conv[li], state_ssm[li], *args)
            c_s.append(cv)
            h_s.append(h)
        wu, wd = w_ff_up[i].astype(BF16), w_ff_down[i].astype(BF16)
        xp = _mlp(xp, wu, wd, ln_ff_g[i], ln_ff_b[i])
        xs = _mlp(xs, wu, wd, ln_ff_g[i], ln_ff_b[i])
    return (xp.reshape(bp, tp, d), xs.reshape(bs, ts, d), jnp.stack(k_p), jnp.stack(v_p), jnp.stack(h_p),
            jnp.stack(c_p), jnp.stack(k_s), jnp.stack(v_s), jnp.stack(h_s), jnp.stack(c_s))
```

```python
import functools
import math

import jax
import jax.numpy as jnp
from jax import lax
from jax.experimental import pallas as pl
from jax.experimental.pallas import tpu as pltpu

F32 = jnp.float32
BF16 = jnp.bfloat16

DEPTH = 4
CHUNK = 64
WIN = 8 * CHUNK
N_HEADS = 16
HEAD_DIM = 64
REL_CLIP = 256
SSM_HEAD_DIM = 64
SSM_GROUPS = 8
D_STATE = 128
CONV_WIDTH = 4
ALPHA = (2 * DEPTH) ** 0.25
LN_EPS = 1e-5
RMS_EPS = 1e-5
NEG_INF = -1e30
LOG2E = math.log2(math.e)

LANES = 128
SUBLANES = 8
VMEM_BYTES_V7X = 64 * 1024 * 1024
VMEM_LIMIT_BYTES = VMEM_BYTES_V7X - 8 * 1024 * 1024

ROW_TILE = 512
ATTN_Q_TILE = 128
ATTN_GROUP = 4
XBC_COL_TILE = 2048
CONV_LANE_CHUNK = 512
MLP_FF_TILE = 2048


def _params(*semantics):
    return pltpu.CompilerParams(dimension_semantics=semantics, vmem_limit_bytes=VMEM_LIMIT_BYTES)


def _const_spec(shape):
    return pl.BlockSpec(shape, lambda *_: (0,) * len(shape), pipeline_mode=pl.Buffered(1))


def _layer_norm(y, g, b):
    mu = jnp.mean(y, axis=-1, keepdims=True)
    d = y - mu
    var = jnp.mean(d * d, axis=-1, keepdims=True)
    return d * lax.rsqrt(var + LN_EPS) * g + b


def _softplus(x):
    return jnp.maximum(x, 0.0) + jnp.log(1.0 + jnp.exp(-jnp.abs(x)))


def _silu(x):
    h = 0.5 * x
    return h + h * jnp.tanh(h)


def _row_tile(rows):
    tm = min(ROW_TILE, rows)
    assert rows % tm == 0
    return tm


def _dense_kernel(x_ref, w_ref, cs_ref, o_ref):
    acc = jnp.dot(x_ref[...].astype(BF16), w_ref[...], preferred_element_type=F32)
    o_ref[...] = (acc * cs_ref[...]).astype(o_ref.dtype)


def _dense(x, w, col_scale, *, out_dtype, row_blocks=None):
    rows, k = x.shape
    n = w.shape[1]
    tm = _row_tile(rows)
    n_rb, rb_fn = row_blocks if row_blocks is not None else (rows // tm, lambda i: i)
    return pl.pallas_call(
        _dense_kernel,
        grid=(n_rb,),
        in_specs=[pl.BlockSpec((tm, k), lambda i: (rb_fn(i), 0)),
                  _const_spec((k, n)),
                  _const_spec((1, n))],
        out_specs=pl.BlockSpec((tm, n), lambda i: (i, 0)),
        out_shape=jax.ShapeDtypeStruct((n_rb * tm, n), out_dtype),
        compiler_params=_params("parallel"),
        name="dense",
    )(x, w, col_scale.reshape(1, n).astype(F32))


def _block_tail_kernel(a_ref, wo_ref, x_ref, g1_ref, b1_ref, wu_ref, wd_ref, g2_ref, b2_ref, o_ref,
                       x1_ref, x1b_ref, acc_ref):
    f = pl.program_id(1)

    @pl.when(f == 0)
    def _():
        mix = jnp.dot(a_ref[...], wo_ref[...], preferred_element_type=F32)
        x1 = _layer_norm(ALPHA * x_ref[...] + mix, g1_ref[...], b1_ref[...])
        x1_ref[...] = x1
        x1b_ref[...] = x1.astype(BF16)

    h = jnp.dot(x1b_ref[...], wu_ref[...], preferred_element_type=F32)
    h = jnp.square(jnp.maximum(h, 0.0)).astype(BF16)
    part = jnp.dot(h, wd_ref[...], preferred_element_type=F32)

    @pl.when(f == 0)
    def _():
        acc_ref[...] = part

    @pl.when(f > 0)
    def _():
        acc_ref[...] += part

    @pl.when(f == pl.num_programs(1) - 1)
    def _():
        o_ref[...] = _layer_norm(ALPHA * x1_ref[...] + acc_ref[...], g2_ref[...], b2_ref[...])


def _block_tail(a, w_o, x, g1, b1, w_up, w_down, g2, b2):
    rows, k = a.shape
    d = x.shape[1]
    d_ff = w_up.shape[1]
    tm = _row_tile(rows)
    tf = min(MLP_FF_TILE, d_ff)
    assert d_ff % tf == 0
    vec = lambda v: v.reshape(1, d).astype(F32)
    return pl.pallas_call(
        _block_tail_kernel,
        grid=(rows // tm, d_ff // tf),
        in_specs=[pl.BlockSpec((tm, k), lambda i, f: (i, 0)),
                  _const_spec((k, d)),
                  pl.BlockSpec((tm, d), lambda i, f: (i, 0)),
                  _const_spec((1, d)), _const_spec((1, d)),
                  pl.BlockSpec((d, tf), lambda i, f: (0, f)),
                  pl.BlockSpec((tf, d), lambda i, f: (f, 0)),
                  _const_spec((1, d)), _const_spec((1, d))],
        out_specs=pl.BlockSpec((tm, d), lambda i, f: (i, 0)),
        out_shape=jax.ShapeDtypeStruct((rows, d), F32),
        scratch_shapes=[pltpu.VMEM((tm, d), F32), pltpu.VMEM((tm, d), BF16), pltpu.VMEM((tm, d), F32)],
        compiler_params=_params("parallel", "arbitrary"),
        name="block_tail",
    )(a, w_o, x, vec(g1), vec(b1), w_up, w_down, vec(g2), vec(b2))


def _attn_kernel(*refs, tq, n_past, has_past):
    if has_past:
        q_ref, k_ref, v_ref, kp_ref, vp_ref, bias_ref, o_ref, kpad, vpad = refs
        kpad[:, 0:WIN] = kp_ref[0].T
        vpad[0:WIN, :] = vp_ref[0]
    else:
        q_ref, k_ref, v_ref, bias_ref, o_ref, kpad, vpad = refs
        kpad[:, 0:WIN] = jnp.zeros((LANES, WIN), BF16)
        vpad[0:WIN, :] = jnp.zeros((WIN, LANES), BF16)
    t = q_ref.shape[1]
    band = WIN + tq
    kpad[:, WIN:WIN + t] = k_ref[0].T
    vpad[WIN:WIN + t, :] = v_ref[0]

    first = lax.broadcasted_iota(jnp.int32, (tq, LANES), 1) < HEAD_DIM
    mask_a = first.astype(BF16)
    mask_b = jnp.logical_not(first).astype(BF16)
    col = lax.broadcasted_iota(jnp.int32, (2 * tq, band), 1)

    n_tiles = t // tq
    n_masked = min(n_tiles, -(-(WIN - n_past) // tq))
    group = min(ATTN_GROUP, n_tiles)
    assert n_tiles % group == 0

    def scores(g):
        out = []
        for r0 in range(g * group * tq, (g + 1) * group * tq, tq):
            q2 = q_ref[0, r0:r0 + tq, :]
            qq = jnp.concatenate([q2 * mask_a, q2 * mask_b], axis=0)
            out.append(jnp.dot(qq, kpad[:, r0:r0 + band], preferred_element_type=F32))
        return out

    def softmax_and_values(g, group_scores):
        probs = []
        for u, s in enumerate(group_scores):
            qs = g * group + u
            s = s + bias_ref[...]
            if qs < n_masked:
                s = jnp.where(col >= (WIN - n_past) - qs * tq, s, NEG_INF)
            p = jnp.exp2(s - jnp.max(s, axis=-1, keepdims=True))
            probs.append((p.astype(BF16), jnp.sum(p, axis=-1, keepdims=True)))
        for u, (p, denom) in enumerate(probs):
            r0 = (g * group + u) * tq
            o = jnp.dot(p, vpad[r0:r0 + band, :], preferred_element_type=F32) / denom
            o_ref[0, r0:r0 + tq, :] = jnp.where(first, o[:tq], o[tq:]).astype(o_ref.dtype)

    n_groups = n_tiles // group
    pending = scores(0)
    for g in range(n_groups):
        upcoming = scores(g + 1) if g + 1 < n_groups else None
        softmax_and_values(g, pending)
        pending = upcoming


def _attn_bias(rel_tables, tq):
    band = WIN + tq
    lo = REL_CLIP + 1 - tq
    assert lo >= 0
    n_diag = tq + band - 1
    n_flat = n_diag - (2 * REL_CLIP + 1 - lo)
    tab = rel_tables.astype(F32)
    by_dist = jnp.concatenate([tab[..., lo:], jnp.repeat(tab[..., -1:], n_flat, axis=-1)], axis=-1)
    rev = by_dist[..., ::-1]
    bias = jnp.stack([rev[..., tq - 1 - r:tq - 1 - r + band] for r in range(tq)], axis=-2)
    r = jnp.arange(tq)[:, None]
    j = jnp.arange(band)[None, :]
    q_chunk = r // CHUNK
    k_chunk = (j - WIN) // CHUNK
    in_band = (k_chunk >= q_chunk - WIN // CHUNK) & (k_chunk <= q_chunk)
    return jnp.where(in_band, bias * LOG2E, NEG_INF)


def _attention(qkv, bias, past=None):
    bsz, t, d3 = qkv.shape
    d = d3 // 3
    n_pairs = d // LANES
    tq = bias.shape[1]
    band = WIN + tq
    assert t % tq == 0 and tq % CHUNK == 0 and bias.shape[2] == band
    has_past = past is not None
    new_spec = lambda off: pl.BlockSpec((1, t, LANES), lambda hp, b: (b, 0, off * n_pairs + hp))
    in_specs = [new_spec(0), new_spec(1), new_spec(2)]
    args = [qkv, qkv, qkv]
    if has_past:
        past_spec = pl.BlockSpec((1, WIN, LANES), lambda hp, b: (b, 0, hp))
        in_specs += [past_spec, past_spec]
        args += list(past)
    in_specs.append(pl.BlockSpec((2 * tq, band), lambda hp, b: (hp, 0)))
    args.append(bias.reshape(N_HEADS * tq, band))
    return pl.pallas_call(
        functools.partial(_attn_kernel, tq=tq, n_past=WIN if has_past else 0, has_past=has_past),
        grid=(n_pairs, bsz),
        in_specs=in_specs,
        out_specs=pl.BlockSpec((1, t, LANES), lambda hp, b: (b, 0, hp)),
        out_shape=jax.ShapeDtypeStruct((bsz, t, d), BF16),
        scratch_shapes=[pltpu.VMEM((LANES, WIN + t), BF16), pltpu.VMEM((WIN + t, LANES), BF16)],
        compiler_params=_params("parallel", "parallel"),
        name="band_attention",
    )(*args)


def _zdt_kernel(x_ref, w_ref, bias_ref, z_ref, dt_ref):
    d_inner = z_ref.shape[1]
    acc = jnp.dot(x_ref[...].astype(BF16), w_ref[...], preferred_element_type=F32)
    z_ref[...] = _silu(acc[:, :d_inner]).astype(z_ref.dtype)
    dt_ref[...] = _softplus(acc[:, d_inner:] + bias_ref[...])


def _gate_and_dt(x, w_zdt, dt_bias_row):
    rows, k = x.shape
    n = w_zdt.shape[1]
    d_inner = n - LANES
    tm = _row_tile(rows)
    return pl.pallas_call(
        _zdt_kernel,
        grid=(rows // tm,),
        in_specs=[pl.BlockSpec((tm, k), lambda i: (i, 0)), _const_spec((k, n)), _const_spec((1, LANES))],
        out_specs=[pl.BlockSpec((tm, d_inner), lambda i: (i, 0)), pl.BlockSpec((tm, LANES), lambda i: (i, 0))],
        out_shape=[jax.ShapeDtypeStruct((rows, d_inner), BF16), jax.ShapeDtypeStruct((rows, LANES), F32)],
        compiler_params=_params("parallel"),
        name="gate_dt",
    )(x, w_zdt, dt_bias_row)


def _xbc_conv_kernel(x_ref, w_ref, st_ref, cw_ref, cb_ref, o_ref, tail_ref, xb_ref, halo_ref):
    i = pl.program_id(1)
    j = pl.program_id(2)
    tt = x_ref.shape[1]

    @pl.when(j == 0)
    def _():
        xb_ref[...] = x_ref[0].astype(BF16)

    @pl.when(i == 0)
    def _():
        halo_ref[j] = st_ref[0]

    tc = min(CONV_LANE_CHUNK, w_ref.shape[1])
    first_sublane = lax.broadcasted_iota(jnp.int32, (1, SUBLANES, tc), 1) == 0
    for c0 in range(0, w_ref.shape[1], tc):
        halo = halo_ref[j, :, c0:c0 + tc]
        xm = jnp.dot(xb_ref[...], w_ref[:, c0:c0 + tc], preferred_element_type=F32)
        taps = [cw_ref[kk, :, c0:c0 + tc] for kk in range(CONV_WIDTH)]
        x3 = xm.reshape(tt // SUBLANES, SUBLANES, tc)
        acc = x3 * taps[0]
        for kk in range(1, CONV_WIDTH):
            before = sum(halo[SUBLANES - 1 - (kk - 1 - m):SUBLANES - (kk - 1 - m), :] * taps[m][0:1, :]
                         for m in range(kk))
            rotated = pltpu.roll(acc, 1, axis=1)
            before8 = jnp.broadcast_to(before, (1, SUBLANES, tc))
            from_prev_group = jnp.concatenate([before8, rotated[:-1]], axis=0)
            acc = jnp.where(first_sublane, from_prev_group, rotated) + x3 * taps[kk]
        h = acc + cb_ref[:, c0:c0 + tc]
        o_ref[0, :, c0:c0 + tc] = (h + h * jnp.tanh(h)).reshape(tt, tc).astype(o_ref.dtype)
        tail = xm[tt - SUBLANES:, :]
        halo_ref[j, :, c0:c0 + tc] = tail
        tail_ref[0, 0, :, c0:c0 + tc] = tail


def _xbc_conv(x, w_xbc, conv_state, conv_w, conv_b):
    bsz, t, k = x.shape
    ch = w_xbc.shape[1]
    tt = min(ROW_TILE, t)
    tn = min(XBC_COL_TILE, ch)
    assert t % tt == 0 and ch % tn == 0 and tt >= SUBLANES
    state8 = jnp.pad(conv_state, ((0, 0), (SUBLANES - (CONV_WIDTH - 1), 0), (0, 0)))
    xa, tail = pl.pallas_call(
        _xbc_conv_kernel,
        grid=(bsz, t // tt, ch // tn),
        in_specs=[pl.BlockSpec((1, tt, k), lambda b, i, j: (b, i, 0)),
                  pl.BlockSpec((k, tn), lambda b, i, j: (0, j)),
                  pl.BlockSpec((1, SUBLANES, tn), lambda b, i, j: (b, 0, j)),
                  pl.BlockSpec((CONV_WIDTH, SUBLANES, tn), lambda b, i, j: (0, 0, j)),
                  pl.BlockSpec((SUBLANES, tn), lambda b, i, j: (0, j))],
        out_specs=[pl.BlockSpec((1, tt, tn), lambda b, i, j: (b, i, j)),
                   pl.BlockSpec((1, 1, SUBLANES, tn), lambda b, i, j: (b, i, 0, j))],
        out_shape=[jax.ShapeDtypeStruct((bsz, t, ch), BF16),
                   jax.ShapeDtypeStruct((bsz, t // tt, SUBLANES, ch), F32)],
        scratch_shapes=[pltpu.VMEM((tt, k), BF16), pltpu.VMEM((ch // tn, SUBLANES, tn), F32)],
        compiler_params=_params("parallel", "arbitrary", "arbitrary"),
        name="xbc_conv",
    )(x, w_xbc, state8,
      jnp.broadcast_to(0.5 * conv_w[:, None, :], (CONV_WIDTH, SUBLANES, ch)),
      jnp.broadcast_to(0.5 * conv_b[None, :], (SUBLANES, ch)))
    return xa, tail[:, -1, SUBLANES - (CONV_WIDTH - 1):, :]


def _dot_f32(a, b):
    return jnp.dot(a, b, preferred_element_type=F32, precision=lax.Precision.HIGHEST)


def _ssd_kernel(x_ref, bm_ref, cm_ref, z_ref, dtc_ref, dtr_ref, ac_ref, ar_ref, dsk_ref, nw_ref,
                h0_ref, y_ref, hout_ref, h_ref):
    c = pl.program_id(1)
    seq = x_ref.shape[1]
    n_pairs = x_ref.shape[2] // LANES
    pairs_per_group = n_pairs // SSM_GROUPS

    @pl.when(c == 0)
    def _():
        h_ref[...] = h0_ref[0]

    ri = lax.broadcasted_iota(jnp.int32, (seq, seq), 0)
    ci = lax.broadcasted_iota(jnp.int32, (seq, seq), 1)
    tri = (ri >= ci).astype(F32)

    cum_c = _dot_f32(tri, dtc_ref[0] * (-jnp.exp(ac_ref[...])))

    pi = lax.broadcasted_iota(jnp.int32, (LANES, LANES), 0)
    qi = lax.broadcasted_iota(jnp.int32, (LANES, LANES), 1)
    same_head = (pi < seq) == (qi < seq)
    tri_bd = (same_head & (pi <= qi)).astype(F32)
    ones_bd = same_head.astype(F32)
    dt_r = dtr_ref[0, 0]
    dta_r = dt_r * (-jnp.exp(ar_ref[...]))
    cum_r = _dot_f32(dta_r, tri_bd)
    last_r = _dot_f32(dta_r, ones_bd)
    w_in_r = jnp.exp(last_r - cum_r) * dt_r

    lane = lax.broadcasted_iota(jnp.int32, (seq, LANES), 1)
    first = lane < SSM_HEAD_DIM
    causal2 = jnp.where(first, lane, lane - seq) <= lax.broadcasted_iota(jnp.int32, (seq, LANES), 0)
    first_rows = lax.broadcasted_iota(jnp.int32, (LANES, LANES), 0) < SSM_HEAD_DIM
    last_c = cum_c[seq - 1:seq, :]

    nt = (((1,), (1,)), ((), ()))
    groups = range(SSM_GROUPS)
    pairs = range(n_pairs)

    bg2 = [jnp.concatenate([bm_ref[0, :, g * D_STATE:(g + 1) * D_STATE]] * 2, axis=0) for g in groups]
    cgs = [cm_ref[0, :, g * D_STATE:(g + 1) * D_STATE] for g in groups]
    cb2 = [lax.dot_general(cgs[g], bg2[g], nt, preferred_element_type=F32) for g in groups]
    h_old = [h_ref[j * LANES:(j + 1) * LANES, :] for j in pairs]
    y_state = [lax.dot_general(cgs[j // pairs_per_group], h_old[j].astype(BF16), nt,
                               preferred_element_type=F32) for j in pairs]

    xpf, col_cum, w2, x_bd, xw_t = [], [], [], [], []
    for j in pairs:
        xf = x_ref[0, :, j * LANES:(j + 1) * LANES].astype(F32)
        cc = jnp.where(first, cum_c[:, 2 * j:2 * j + 1], cum_c[:, 2 * j + 1:2 * j + 2])
        decay = jnp.exp(jnp.where(causal2, cc - cum_r[j:j + 1, :], -jnp.inf))
        bd = jnp.concatenate([jnp.where(first, xf, 0.0), jnp.where(first, 0.0, xf)], axis=0)
        xpf.append(xf)
        col_cum.append(cc)
        w2.append((cb2[j // pairs_per_group] * decay * dt_r[j:j + 1, :]).astype(BF16))
        x_bd.append(bd.astype(BF16))
        xw_t.append((bd.T * w_in_r[j:j + 1, :]).astype(BF16))

    y_intra = [jnp.dot(w2[j], x_bd[j], preferred_element_type=F32) for j in pairs]
    upd = [jnp.dot(xw_t[j], bg2[j // pairs_per_group], preferred_element_type=F32) for j in pairs]

    for j in pairs:
        e_last = jnp.exp(jnp.where(first_rows, last_c[:, 2 * j:2 * j + 1], last_c[:, 2 * j + 1:2 * j + 2]))
        h_ref[j * LANES:(j + 1) * LANES, :] = h_old[j] * e_last + upd[j]
    for g in groups:
        ys = [y_intra[j] + jnp.exp(col_cum[j]) * y_state[j] + dsk_ref[:, j * LANES:(j + 1) * LANES] * xpf[j]
              for j in range(g * pairs_per_group, (g + 1) * pairs_per_group)]
        glo = g * pairs_per_group * LANES
        ghi = glo + pairs_per_group * LANES
        yg = jnp.concatenate(ys, axis=1) * z_ref[0, :, glo:ghi].astype(F32)
        ms = jnp.mean(yg * yg, axis=-1, keepdims=True)
        y_ref[0, :, glo:ghi] = (yg * lax.rsqrt(ms + RMS_EPS) * nw_ref[:, glo:ghi]).astype(y_ref.dtype)

    @pl.when(c == pl.num_programs(1) - 1)
    def _():
        hout_ref[0] = h_ref[...]


def _ssd(xa, z, dt, h0, a_log, d_skip, norm_w):
    bsz, t, _ = xa.shape
    d_inner = z.shape[2]
    n_heads = d_inner // SSM_HEAD_DIM
    n_pairs = n_heads // 2
    gn = SSM_GROUPS * D_STATE
    assert t % CHUNK == 0 and d_inner % gn == 0
    nc = t // CHUNK
    dt_rows = dt[:, :, :n_heads].reshape(bsz, nc, CHUNK, n_pairs, 2)
    dt_rows = dt_rows.transpose(0, 1, 3, 4, 2).reshape(bsz, nc, n_pairs, 2 * CHUNK)
    a_col = jnp.pad(a_log, (0, LANES - n_heads)).reshape(1, LANES)
    a_rows = jnp.repeat(a_log, CHUNK).reshape(n_pairs, 2 * CHUNK)
    d_skip_row = jnp.repeat(d_skip, SSM_HEAD_DIM).reshape(1, d_inner)
    x_blocks = d_inner // gn
    y, h_out = pl.pallas_call(
        _ssd_kernel,
        grid=(bsz, nc),
        in_specs=[pl.BlockSpec((1, CHUNK, d_inner), lambda b, c: (b, c, 0)),
                  pl.BlockSpec((1, CHUNK, gn), lambda b, c: (b, c, x_blocks)),
                  pl.BlockSpec((1, CHUNK, gn), lambda b, c: (b, c, x_blocks + 1)),
                  pl.BlockSpec((1, CHUNK, d_inner), lambda b, c: (b, c, 0)),
                  pl.BlockSpec((1, CHUNK, LANES), lambda b, c: (b, c, 0)),
                  pl.BlockSpec((1, 1, n_pairs, LANES), lambda b, c: (b, c, 0, 0)),
                  _const_spec((1, LANES)),
                  _const_spec((n_pairs, LANES)),
                  _const_spec((1, d_inner)),
                  _const_spec((1, d_inner)),
                  pl.BlockSpec((1, d_inner, D_STATE), lambda b, c: (b, 0, 0))],
        out_specs=[pl.BlockSpec((1, CHUNK, d_inner), lambda b, c: (b, c, 0)),
                   pl.BlockSpec((1, d_inner, D_STATE), lambda b, c: (b, 0, 0))],
        out_shape=[jax.ShapeDtypeStruct((bsz, t, d_inner), BF16),
                   jax.ShapeDtypeStruct((bsz, d_inner, D_STATE), F32)],
        scratch_shapes=[pltpu.VMEM((d_inner, D_STATE), F32)],
        compiler_params=_params("parallel", "arbitrary"),
        name="ssd",
    )(xa, xa, xa, z, dt, dt_rows, a_col, a_rows, d_skip_row, norm_w.reshape(1, d_inner),
      h0.reshape(bsz, d_inner, D_STATE))
    return y, h_out.reshape(bsz, n_heads, SSM_HEAD_DIM, D_STATE)


def _attention_mixer(x, bsz, w_qkv, q_scale, bias, past):
    rows, d = x.shape
    t = rows // bsz
    qkv = _dense(x, w_qkv, q_scale, out_dtype=BF16)
    keep = min(WIN, t)
    kept = None
    if keep < t:
        tm = _row_tile(rows)
        assert t % tm == 0 and keep % tm == 0
        blocks_per_seq, blocks_kept = t // tm, keep // tm
        kept = (bsz * blocks_kept,
                lambda i: (i // blocks_kept) * blocks_per_seq + blocks_per_seq - blocks_kept + i % blocks_kept)
    kv = _dense(x, w_qkv[:, d:], jnp.ones((2 * d,), F32), out_dtype=F32, row_blocks=kept)
    k_new = kv[:, :d].reshape(bsz, keep, N_HEADS, HEAD_DIM)
    v_new = kv[:, d:].reshape(bsz, keep, N_HEADS, HEAD_DIM)
    if past is not None:
        past = tuple(p.reshape(bsz, WIN, d).astype(BF16) for p in past)
    o = _attention(qkv.reshape(bsz, t, 3 * d), bias, past)
    return o.reshape(rows, d), k_new, v_new


def _ssm_mixer(x, bsz, conv_state, ssm_state, w_zdt, w_xbc, dt_bias_row, conv_w, conv_b, a_log, d_skip,
               norm_w):
    rows, d = x.shape
    t = rows // bsz
    z, dt = _gate_and_dt(x, w_zdt, dt_bias_row)
    xa, new_conv = _xbc_conv(x.reshape(bsz, t, d), w_xbc, conv_state, conv_w, conv_b)
    d_inner = z.shape[1]
    y, h_new = _ssd(xa, z.reshape(bsz, t, d_inner), dt.reshape(bsz, t, LANES), ssm_state, a_log, d_skip,
                    norm_w)
    return y.reshape(rows, d_inner), new_conv, h_new


def kernel(x_prompt, x_sample, cache_k, cache_v, state_ssm, state_conv, w_qkv, rel_bias, w_attn_out,
           w_ssm_in, conv_w, conv_b, dt_bias, a_log, d_skip, ssm_norm_w, w_ssm_out, ln_mix_g, ln_mix_b,
           w_ff_up, w_ff_down, ln_ff_g, ln_ff_b):
    bp, tp, d = x_prompt.shape
    bs, ts, _ = x_sample.shape
    n_ssm_heads = dt_bias.shape[1]
    d_inner = ssm_norm_w.shape[1]
    conv_ch = conv_w.shape[2]
    xp = x_prompt.reshape(bp * tp, d)
    xs = x_sample.reshape(bs * ts, d)
    tq_p, tq_s = min(ATTN_Q_TILE, tp), min(ATTN_Q_TILE, ts)
    bias_p = _attn_bias(rel_bias, tq_p)
    bias_s = bias_p[:, :, :tq_s, :WIN + tq_s] if tq_s < tq_p else bias_p
    q_scale = jnp.concatenate([jnp.full((d,), HEAD_DIM ** -0.5 * LOG2E, F32), jnp.ones((2 * d,), F32)])
    k_p, v_p, k_s, v_s, h_p, c_p, h_s, c_s = ([] for _ in range(8))
    for i in range(DEPTH):
        li = i // 2
        if i % 2 == 0:
            wq, wo = w_qkv[li].astype(BF16), w_attn_out[li].astype(BF16)
            mp, k, v = _attention_mixer(xp, bp, wq, q_scale, bias_p[li], None)
            k_p.append(k)
            v_p.append(v)
            ms, k, v = _attention_mixer(xs, bs, wq, q_scale, bias_s[li], (cache_k[li], cache_v[li]))
            k_s.append(k)
            v_s.append(v)
        else:
            w_in = w_ssm_in[li].astype(BF16)
            w_zdt = jnp.concatenate([w_in[:, :d_inner],
                                     jnp.pad(w_in[:, d_inner + conv_ch:], ((0, 0), (0, LANES - n_ssm_heads)))],
                                    axis=1)
            w_xbc = w_in[:, d_inner:d_inner + conv_ch]
            dt_bias_row = jnp.pad(dt_bias[li], (0, LANES - n_ssm_heads)).reshape(1, LANES)
            args = (w_zdt, w_xbc, dt_bias_row, conv_w[li], conv_b[li], a_log[li], d_skip[li], ssm_norm_w[li])
            wo = w_ssm_out[li].astype(BF16)
            conv0 = jnp.zeros((bp, CONV_WIDTH - 1, conv_ch), F32)
            h00 = jnp.zeros((bp, n_ssm_heads, SSM_HEAD_DIM, D_STATE), F32)
            mp, cv, h = _ssm_mixer(xp, bp, conv0, h00, *args)
            c_p.append(cv)
            h_p.append(h)
            ms, cv, h = _ssm_mixer(xs, bs, state_conv[li], state_ssm[li], *args)
            c_s.append(cv)
            h_s.append(h)
        tail = (ln_mix_g[i], ln_mix_b[i], w_ff_up[i].astype(BF16), w_ff_down[i].astype(BF16),
                ln_ff_g[i], ln_ff_b[i])
        xp = _block_tail(mp, wo, xp, *tail)
        xs = _block_tail(ms, wo, xs, *tail)
    return (xp.reshape(bp, tp, d), xs.reshape(bs, ts, d), jnp.stack(k_p), jnp.stack(v_p), jnp.stack(h_p),
            jnp.stack(c_p), jnp.stack(k_s), jnp.stack(v_s), jnp.stack(h_s), jnp.stack(c_s))
```

```python
import functools
import math

import jax
import jax.numpy as jnp
from jax import lax
from jax.experimental import pallas as pl
from jax.experimental.pallas import tpu as pltpu

F32 = jnp.float32
BF16 = jnp.bfloat16

DEPTH = 4
CHUNK = 64
WIN = 8 * CHUNK
N_HEADS = 16
HEAD_DIM = 64
REL_CLIP = 256
SSM_HEAD_DIM = 64
SSM_GROUPS = 8
D_STATE = 128
CONV_WIDTH = 4
ALPHA = (2 * DEPTH) ** 0.25
LN_EPS = 1e-5
RMS_EPS = 1e-5
NEG_INF = -1e30
LOG2E = math.log2(math.e)

LANES = 128
SUBLANES = 8
VMEM_BYTES_V7X = 64 * 1024 * 1024
VMEM_LIMIT_BYTES = VMEM_BYTES_V7X - 8 * 1024 * 1024

ROW_TILE = 512
ATTN_Q_TILE = 128
ATTN_GROUP = 4
ATTN_STEP_ROWS = 512
XBC_COL_TILE = 2048
CONV_LANE_CHUNK = 256
MLP_FF_TILE = 2048


def _params(*semantics):
    return pltpu.CompilerParams(dimension_semantics=semantics, vmem_limit_bytes=VMEM_LIMIT_BYTES)


def _const_spec(shape):
    return pl.BlockSpec(shape, lambda *_: (0,) * len(shape), pipeline_mode=pl.Buffered(1))


def _layer_norm(y, g, b):
    mu = jnp.mean(y, axis=-1, keepdims=True)
    d = y - mu
    var = jnp.mean(d * d, axis=-1, keepdims=True)
    return d * lax.rsqrt(var + LN_EPS) * g + b


def _softplus(x):
    return jnp.maximum(x, 0.0) + jnp.log(1.0 + jnp.exp(-jnp.abs(x)))


def _silu(x):
    h = 0.5 * x
    return h + h * jnp.tanh(h)


def _row_tile(rows):
    tm = min(ROW_TILE, rows)
    assert rows % tm == 0
    return tm


def _dense_kernel(x_ref, w_ref, cs_ref, o_ref):
    acc = jnp.dot(x_ref[...].astype(BF16), w_ref[...], preferred_element_type=F32)
    o_ref[...] = (acc * cs_ref[...]).astype(o_ref.dtype)


def _dense(x, w, col_scale, *, out_dtype, row_blocks=None):
    rows, k = x.shape
    n = w.shape[1]
    tm = _row_tile(rows)
    n_rb, rb_fn = row_blocks if row_blocks is not None else (rows // tm, lambda i: i)
    return pl.pallas_call(
        _dense_kernel,
        grid=(n_rb,),
        in_specs=[pl.BlockSpec((tm, k), lambda i: (rb_fn(i), 0)),
                  _const_spec((k, n)),
                  _const_spec((1, n))],
        out_specs=pl.BlockSpec((tm, n), lambda i: (i, 0)),
        out_shape=jax.ShapeDtypeStruct((n_rb * tm, n), out_dtype),
        compiler_params=_params("parallel"),
        name="dense",
    )(x, w, col_scale.reshape(1, n).astype(F32))


def _block_tail_kernel(a_ref, wo_ref, x_ref, g1_ref, b1_ref, wu_ref, wd_ref, g2_ref, b2_ref, o_ref,
                       mix_ref, x1_ref, acc_ref):
    @pl.when(pl.program_id(0) == 0)
    def _():
        mix_ref[...] = jnp.zeros_like(mix_ref)
        x1_ref[...] = jnp.zeros_like(x1_ref)
        acc_ref[...] = jnp.zeros_like(acc_ref)

    out = _layer_norm(ALPHA * x1_ref[...] + acc_ref[...], g2_ref[...], b2_ref[...])
    mix = jnp.dot(a_ref[...], wo_ref[...], preferred_element_type=F32)
    x1 = _layer_norm(ALPHA * x_ref[...] + mix_ref[...], g1_ref[...], b1_ref[...])
    x1b = x1.astype(BF16)
    d_ff = wu_ref.shape[1]
    tf = min(MLP_FF_TILE, d_ff)
    acc = None
    for f0 in range(0, d_ff, tf):
        h = jnp.dot(x1b, wu_ref[:, f0:f0 + tf], preferred_element_type=F32)
        h = jnp.square(jnp.maximum(h, 0.0)).astype(BF16)
        part = jnp.dot(h, wd_ref[f0:f0 + tf, :], preferred_element_type=F32)
        acc = part if acc is None else acc + part
    o_ref[...] = out
    mix_ref[...] = mix
    x1_ref[...] = x1
    acc_ref[...] = acc


def _block_tail(a, w_o, x, g1, b1, w_up, w_down, g2, b2):
    rows, k = a.shape
    d = x.shape[1]
    d_ff = w_up.shape[1]
    tm = _row_tile(rows)
    n_tiles = rows // tm
    last = n_tiles - 1
    vec = lambda v: v.reshape(1, d).astype(F32)
    return pl.pallas_call(
        _block_tail_kernel,
        grid=(n_tiles + 2,),
        in_specs=[pl.BlockSpec((tm, k), lambda i: (jnp.minimum(i, last), 0)),
                  _const_spec((k, d)),
                  pl.BlockSpec((tm, d), lambda i: (jnp.clip(i - 1, 0, last), 0)),
                  _const_spec((1, d)), _const_spec((1, d)),
                  _const_spec((d, d_ff)),
                  _const_spec((d_ff, d)),
                  _const_spec((1, d)), _const_spec((1, d))],
        out_specs=pl.BlockSpec((tm, d), lambda i: (jnp.maximum(i - 2, 0), 0)),
        out_shape=jax.ShapeDtypeStruct((rows, d), F32),
        scratch_shapes=[pltpu.VMEM((tm, d), F32), pltpu.VMEM((tm, d), F32), pltpu.VMEM((tm, d), F32)],
        compiler_params=_params("arbitrary"),
        name="block_tail",
    )(a, w_o, x, vec(g1), vec(b1), w_up, w_down, vec(g2), vec(b2))


def _attn_kernel(*refs, tq, n_past, has_past):
    if has_past:
        q_ref, k_ref, v_ref, kp_ref, vp_ref, bias_ref, o_ref, kpad, vpad = refs
    else:
        q_ref, k_ref, v_ref, bias_ref, o_ref, kpad, vpad = refs
    n_seq, t = q_ref.shape[0], q_ref.shape[1]
    band = WIN + tq
    for b in range(n_seq):
        if has_past:
            kpad[b, :, 0:WIN] = kp_ref[0, b].astype(BF16)
            vpad[b, 0:WIN, :] = vp_ref[0, b].T.astype(BF16)
        else:
            kpad[b, :, 0:WIN] = jnp.zeros((LANES, WIN), BF16)
            vpad[b, 0:WIN, :] = jnp.zeros((WIN, LANES), BF16)
        kpad[b, :, WIN:WIN + t] = k_ref[b].T
        vpad[b, WIN:WIN + t, :] = v_ref[b]

    first = lax.broadcasted_iota(jnp.int32, (tq, LANES), 1) < HEAD_DIM
    mask_a = first.astype(BF16)
    mask_b = jnp.logical_not(first).astype(BF16)
    col = lax.broadcasted_iota(jnp.int32, (2 * tq, band), 1)

    n_masked = min(t // tq, -(-(WIN - n_past) // tq))
    units = [(b, qs) for b in range(n_seq) for qs in range(t // tq)]
    group = min(ATTN_GROUP, len(units))
    assert len(units) % group == 0

    def scores(g):
        out = []
        for b, qs in units[g * group:(g + 1) * group]:
            r0 = qs * tq
            q2 = q_ref[b, r0:r0 + tq, :]
            qq = jnp.concatenate([q2 * mask_a, q2 * mask_b], axis=0)
            out.append(jnp.dot(qq, kpad[b, :, r0:r0 + band], preferred_element_type=F32))
        return out

    def softmax_and_values(g, group_scores):
        probs = []
        for (b, qs), s in zip(units[g * group:(g + 1) * group], group_scores):
            s = s + bias_ref[...]
            if qs < n_masked:
                s = jnp.where(col >= (WIN - n_past) - qs * tq, s, NEG_INF)
            p = jnp.exp2(s - jnp.max(s, axis=-1, keepdims=True))
            probs.append((p.astype(BF16), jnp.sum(p, axis=-1, keepdims=True)))
        for (b, qs), (p, denom) in zip(units[g * group:(g + 1) * group], probs):
            r0 = qs * tq
            o = jnp.dot(p, vpad[b, r0:r0 + band, :], preferred_element_type=F32) / denom
            o_ref[b, r0:r0 + tq, :] = jnp.where(first, o[:tq], o[tq:]).astype(o_ref.dtype)

    n_groups = len(units) // group
    pending = scores(0)
    for g in range(n_groups):
        upcoming = scores(g + 1) if g + 1 < n_groups else None
        softmax_and_values(g, pending)
        pending = upcoming


def _attn_bias(rel_tables, tq):
    band = WIN + tq
    lo = REL_CLIP + 1 - tq
    assert lo >= 0
    n_diag = tq + band - 1
    n_flat = n_diag - (2 * REL_CLIP + 1 - lo)
    tab = rel_tables.astype(F32)
    by_dist = jnp.concatenate([tab[..., lo:], jnp.repeat(tab[..., -1:], n_flat + 1, axis=-1)], axis=-1)
    lead = by_dist.shape[:-1]
    skew = jnp.tile(by_dist, (1,) * len(lead) + (tq,))[..., :tq * n_diag].reshape(lead + (tq, n_diag))
    bias = skew[..., ::-1, ::-1][..., :band]
    r = jnp.arange(tq)[:, None]
    j = jnp.arange(band)[None, :]
    q_chunk = r // CHUNK
    k_chunk = (j - WIN) // CHUNK
    in_band = (k_chunk >= q_chunk - WIN // CHUNK) & (k_chunk <= q_chunk)
    return jnp.where(in_band, bias * LOG2E, NEG_INF)


def _attention(qkv, bias, past=None, layer=0):
    bsz, t, d3 = qkv.shape
    d = d3 // 3
    n_pairs = d // LANES
    tq = bias.shape[1]
    band = WIN + tq
    assert t % tq == 0 and tq % CHUNK == 0 and bias.shape[2] == band
    has_past = past is not None
    n_seq = max(1, min(bsz, ATTN_STEP_ROWS // t))
    assert bsz % n_seq == 0
    new_spec = lambda off: pl.BlockSpec((n_seq, t, LANES), lambda hp, b: (b, 0, off * n_pairs + hp))
    in_specs = [new_spec(0), new_spec(1), new_spec(2)]
    args = [qkv, qkv, qkv]
    if has_past:
        past_spec = pl.BlockSpec((1, n_seq, LANES, WIN), lambda hp, b: (layer, b, hp, 0))
        in_specs += [past_spec, past_spec]
        args += list(past)
    in_specs.append(pl.BlockSpec((2 * tq, band), lambda hp, b: (hp, 0)))
    args.append(bias.reshape(N_HEADS * tq, band))
    return pl.pallas_call(
        functools.partial(_attn_kernel, tq=tq, n_past=WIN if has_past else 0, has_past=has_past),
        grid=(n_pairs, bsz // n_seq),
        in_specs=in_specs,
        out_specs=pl.BlockSpec((n_seq, t, LANES), lambda hp, b: (b, 0, hp)),
        out_shape=jax.ShapeDtypeStruct((bsz, t, d), BF16),
        scratch_shapes=[pltpu.VMEM((n_seq, LANES, WIN + t), BF16), pltpu.VMEM((n_seq, WIN + t, LANES), BF16)],
        compiler_params=_params("parallel", "parallel"),
        name="band_attention",
    )(*args)


def _zdt_kernel(x_ref, w_ref, bias_ref, z_ref, dt_ref):
    d_inner = z_ref.shape[1]
    acc = jnp.dot(x_ref[...].astype(BF16), w_ref[...], preferred_element_type=F32)
    z_ref[...] = _silu(acc[:, :d_inner]).astype(z_ref.dtype)
    dt_ref[...] = _softplus(acc[:, d_inner:] + bias_ref[...])


def _gate_and_dt(x, w_zdt, dt_bias_row):
    rows, k = x.shape
    n = w_zdt.shape[1]
    d_inner = n - LANES
    tm = _row_tile(rows)
    return pl.pallas_call(
        _zdt_kernel,
        grid=(rows // tm,),
        in_specs=[pl.BlockSpec((tm, k), lambda i: (i, 0)), _const_spec((k, n)), _const_spec((1, LANES))],
        out_specs=[pl.BlockSpec((tm, d_inner), lambda i: (i, 0)), pl.BlockSpec((tm, LANES), lambda i: (i, 0))],
        out_shape=[jax.ShapeDtypeStruct((rows, d_inner), BF16), jax.ShapeDtypeStruct((rows, LANES), F32)],
        compiler_params=_params("parallel"),
        name="gate_dt",
    )(x, w_zdt, dt_bias_row)


def _xbc_tile(s, nt, nc):
    return s // (nt * nc), (s // nc) % nt, s % nc


def _xbc_conv_kernel(x_ref, w_ref, st_ref, cw_ref, cb_ref, o_ref, tail_ref, xb_ref, halo_ref, xm_ref,
                     *, nt, nc):
    s = pl.program_id(0)
    n_tiles = pl.num_programs(0) - 1
    j_mm = jnp.minimum(s, n_tiles - 1) % nc
    _, i, j = _xbc_tile(jnp.maximum(s - 1, 0), nt, nc)
    tt = x_ref.shape[1]

    @pl.when(s == 0)
    def _():
        xm_ref[...] = jnp.zeros_like(xm_ref)

    @pl.when(j_mm == 0)
    def _():
        xb_ref[...] = x_ref[0].astype(BF16)

    @pl.when(i == 0)
    def _():
        halo_ref[j] = st_ref[0]

    tc = min(CONV_LANE_CHUNK, w_ref.shape[1])
    first_sublane = lax.broadcasted_iota(jnp.int32, (1, SUBLANES, tc), 1) == 0
    for c0 in range(0, w_ref.shape[1], tc):
        xm = xm_ref[:, c0:c0 + tc]
        xm_ref[:, c0:c0 + tc] = jnp.dot(xb_ref[...], w_ref[:, c0:c0 + tc], preferred_element_type=F32)
        halo = halo_ref[j, :, c0:c0 + tc]
        taps = [cw_ref[kk, :, c0:c0 + tc] for kk in range(CONV_WIDTH)]
        x3 = xm.reshape(tt // SUBLANES, SUBLANES, tc)
        acc = x3 * taps[0]
        for kk in range(1, CONV_WIDTH):
            before = sum(halo[SUBLANES - 1 - (kk - 1 - m):SUBLANES - (kk - 1 - m), :] * taps[m][0:1, :]
                         for m in range(kk))
            rotated = pltpu.roll(acc, 1, axis=1)
            before8 = jnp.broadcast_to(before, (1, SUBLANES, tc))
            from_prev_group = jnp.concatenate([before8, rotated[:-1]], axis=0)
            acc = jnp.where(first_sublane, from_prev_group, rotated) + x3 * taps[kk]
        h = acc + cb_ref[:, c0:c0 + tc]
        o_ref[0, :, c0:c0 + tc] = (h + h * jnp.tanh(h)).reshape(tt, tc).astype(o_ref.dtype)
        tail = xm[tt - SUBLANES:, :]
        halo_ref[j, :, c0:c0 + tc] = tail
        tail_ref[0, 0, :, c0:c0 + tc] = tail


def _xbc_conv(x, w_xbc, conv_state, conv_w, conv_b):
    bsz, t, k = x.shape
    ch = w_xbc.shape[1]
    tt = min(ROW_TILE, t)
    tn = min(XBC_COL_TILE, ch)
    assert t % tt == 0 and ch % tn == 0 and tt >= SUBLANES
    state8 = jnp.pad(conv_state, ((0, 0), (SUBLANES - (CONV_WIDTH - 1), 0), (0, 0)))
    nt, nc = t // tt, ch // tn
    n_tiles = bsz * nt * nc
    proj = lambda s: _xbc_tile(jnp.minimum(s, n_tiles - 1), nt, nc)
    conv = lambda s: _xbc_tile(jnp.maximum(s - 1, 0), nt, nc)
    xa, tail = pl.pallas_call(
        functools.partial(_xbc_conv_kernel, nt=nt, nc=nc),
        grid=(n_tiles + 1,),
        in_specs=[pl.BlockSpec((1, tt, k), lambda s: (proj(s)[0], proj(s)[1], 0)),
                  pl.BlockSpec((k, tn), lambda s: (0, proj(s)[2])),
                  pl.BlockSpec((1, SUBLANES, tn), lambda s: (conv(s)[0], 0, conv(s)[2])),
                  pl.BlockSpec((CONV_WIDTH, SUBLANES, tn), lambda s: (0, 0, conv(s)[2])),
                  pl.BlockSpec((SUBLANES, tn), lambda s: (0, conv(s)[2]))],
        out_specs=[pl.BlockSpec((1, tt, tn), lambda s: conv(s)),
                   pl.BlockSpec((1, 1, SUBLANES, tn), lambda s: (conv(s)[0], conv(s)[1], 0, conv(s)[2]))],
        out_shape=[jax.ShapeDtypeStruct((bsz, t, ch), BF16),
                   jax.ShapeDtypeStruct((bsz, nt, SUBLANES, ch), F32)],
        scratch_shapes=[pltpu.VMEM((tt, k), BF16), pltpu.VMEM((nc, SUBLANES, tn), F32),
                        pltpu.VMEM((tt, tn), F32)],
        compiler_params=_params("arbitrary"),
        name="xbc_conv",
    )(x, w_xbc, state8,
      jnp.broadcast_to(0.5 * conv_w[:, None, :], (CONV_WIDTH, SUBLANES, ch)),
      jnp.broadcast_to(0.5 * conv_b[None, :], (SUBLANES, ch)))
    return xa, tail[:, -1, SUBLANES - (CONV_WIDTH - 1):, :]


def _dot_f32(a, b):
    return jnp.dot(a, b, preferred_element_type=F32, precision=lax.Precision.HIGHEST)


def _ssd_kernel(x_ref, bm_ref, cm_ref, z_ref, dtc_ref, dtr_ref, ac_ref, ar_ref, dsk_ref, nw_ref,
                *refs, has_h0):
    y_ref, hout_ref, h_ref = refs[-3:]
    c = pl.program_id(1)
    seq = x_ref.shape[1]
    n_pairs = x_ref.shape[2] // LANES
    pairs_per_group = n_pairs // SSM_GROUPS

    @pl.when(c == 0)
    def _():
        h_ref[...] = refs[0][0, 0] if has_h0 else jnp.zeros_like(h_ref)

    ri = lax.broadcasted_iota(jnp.int32, (seq, seq), 0)
    ci = lax.broadcasted_iota(jnp.int32, (seq, seq), 1)
    tri = (ri >= ci).astype(F32)

    cum_c = _dot_f32(tri, dtc_ref[0] * (-jnp.exp(ac_ref[...])))

    pi = lax.broadcasted_iota(jnp.int32, (LANES, LANES), 0)
    qi = lax.broadcasted_iota(jnp.int32, (LANES, LANES), 1)
    same_head = (pi < seq) == (qi < seq)
    tri_bd = (same_head & (pi <= qi)).astype(F32)
    ones_bd = same_head.astype(F32)
    dt_r = dtr_ref[0, 0]
    dta_r = dt_r * (-jnp.exp(ar_ref[...]))
    cum_r = _dot_f32(dta_r, tri_bd)
    last_r = _dot_f32(dta_r, ones_bd)
    w_in_r = jnp.exp(last_r - cum_r) * dt_r

    lane = lax.broadcasted_iota(jnp.int32, (seq, LANES), 1)
    first = lane < SSM_HEAD_DIM
    causal2 = jnp.where(first, lane, lane - seq) <= lax.broadcasted_iota(jnp.int32, (seq, LANES), 0)
    first_rows = lax.broadcasted_iota(jnp.int32, (LANES, LANES), 0) < SSM_HEAD_DIM
    e_last_c = jnp.exp(cum_c[seq - 1:seq, :])

    nt = (((1,), (1,)), ((), ()))
    groups = range(SSM_GROUPS)
    pairs = range(n_pairs)

    bg2 = [jnp.concatenate([bm_ref[0, :, g * D_STATE:(g + 1) * D_STATE]] * 2, axis=0) for g in groups]
    cgs = [cm_ref[0, :, g * D_STATE:(g + 1) * D_STATE] for g in groups]
    cb2 = [lax.dot_general(cgs[g], bg2[g], nt, preferred_element_type=F32) for g in groups]
    h_old = [h_ref[j * LANES:(j + 1) * LANES, :] for j in pairs]
    y_state = [lax.dot_general(cgs[j // pairs_per_group], h_old[j].astype(BF16), nt,
                               preferred_element_type=F32) for j in pairs]

    xpf, col_cum, w2, x_bd, xw_t = [], [], [], [], []
    for j in pairs:
        xf = x_ref[0, :, j * LANES:(j + 1) * LANES].astype(F32)
        cc = jnp.where(first, cum_c[:, 2 * j:2 * j + 1], cum_c[:, 2 * j + 1:2 * j + 2])
        decay = jnp.exp(jnp.where(causal2, cc - cum_r[j:j + 1, :], -jnp.inf))
        bd = jnp.concatenate([jnp.where(first, xf, 0.0), jnp.where(first, 0.0, xf)], axis=0)
        xpf.append(xf)
        col_cum.append(cc)
        w2.append((cb2[j // pairs_per_group] * decay * dt_r[j:j + 1, :]).astype(BF16))
        x_bd.append(bd.astype(BF16))
        xw_t.append((bd.T * w_in_r[j:j + 1, :]).astype(BF16))

    y_intra = [jnp.dot(w2[j], x_bd[j], preferred_element_type=F32) for j in pairs]
    upd = [jnp.dot(xw_t[j], bg2[j // pairs_per_group], preferred_element_type=F32) for j in pairs]

    for j in pairs:
        e_last = jnp.where(first_rows, e_last_c[:, 2 * j:2 * j + 1], e_last_c[:, 2 * j + 1:2 * j + 2])
        h_ref[j * LANES:(j + 1) * LANES, :] = h_old[j] * e_last + upd[j]
    for g in groups:
        ys = [y_intra[j] + jnp.exp(col_cum[j]) * y_state[j] + dsk_ref[:, j * LANES:(j + 1) * LANES] * xpf[j]
              for j in range(g * pairs_per_group, (g + 1) * pairs_per_group)]
        glo = g * pairs_per_group * LANES
        ghi = glo + pairs_per_group * LANES
        yg = jnp.concatenate(ys, axis=1) * z_ref[0, :, glo:ghi].astype(F32)
        ms = jnp.mean(yg * yg, axis=-1, keepdims=True)
        y_ref[0, :, glo:ghi] = (yg * lax.rsqrt(ms + RMS_EPS) * nw_ref[:, glo:ghi]).astype(y_ref.dtype)

    @pl.when(c == pl.num_programs(1) - 1)
    def _():
        hout_ref[0, 0] = h_ref[...]


def _ssd(xa, z, dt, h0, a_log, d_skip, norm_w, layer, n_layers, states):
    bsz, t, _ = xa.shape
    d_inner = z.shape[2]
    n_heads = d_inner // SSM_HEAD_DIM
    n_pairs = n_heads // 2
    gn = SSM_GROUPS * D_STATE
    assert t % CHUNK == 0 and d_inner % gn == 0
    nc = t // CHUNK
    dt_rows = dt[:, :, :n_heads].reshape(bsz, nc, CHUNK, n_pairs, 2)
    dt_rows = dt_rows.transpose(0, 1, 3, 4, 2).reshape(bsz, nc, n_pairs, 2 * CHUNK)
    a_col = jnp.pad(a_log, (0, LANES - n_heads)).reshape(1, LANES)
    a_rows = jnp.repeat(a_log, CHUNK).reshape(n_pairs, 2 * CHUNK)
    d_skip_row = jnp.repeat(d_skip, SSM_HEAD_DIM).reshape(1, d_inner)
    x_blocks = d_inner // gn
    state_spec = pl.BlockSpec((1, 1, d_inner, D_STATE), lambda b, c: (layer, b, 0, 0))
    in_specs = [pl.BlockSpec((1, CHUNK, d_inner), lambda b, c: (b, c, 0)),
                pl.BlockSpec((1, CHUNK, gn), lambda b, c: (b, c, x_blocks)),
                pl.BlockSpec((1, CHUNK, gn), lambda b, c: (b, c, x_blocks + 1)),
                pl.BlockSpec((1, CHUNK, d_inner), lambda b, c: (b, c, 0)),
                pl.BlockSpec((1, CHUNK, LANES), lambda b, c: (b, c, 0)),
                pl.BlockSpec((1, 1, n_pairs, LANES), lambda b, c: (b, c, 0, 0)),
                _const_spec((1, LANES)),
                _const_spec((n_pairs, LANES)),
                _const_spec((1, d_inner)),
                _const_spec((1, d_inner))]
    args = [xa, xa, xa, z, dt, dt_rows, a_col, a_rows, d_skip_row, norm_w.reshape(1, d_inner)]
    if h0 is not None:
        in_specs.append(state_spec)
        args.append(h0.reshape(n_layers, bsz, d_inner, D_STATE))
    aliases = {}
    if states is not None:
        aliases = {len(args): 1}
        in_specs.append(pl.BlockSpec(memory_space=pl.ANY))
        args.append(states)
    return pl.pallas_call(
        functools.partial(_ssd_kernel, has_h0=h0 is not None),
        grid=(bsz, nc),
        in_specs=in_specs,
        out_specs=[pl.BlockSpec((1, CHUNK, d_inner), lambda b, c: (b, c, 0)), state_spec],
        out_shape=[jax.ShapeDtypeStruct((bsz, t, d_inner), BF16),
                   jax.ShapeDtypeStruct((n_layers, bsz, d_inner, D_STATE), F32)],
        scratch_shapes=[pltpu.VMEM((d_inner, D_STATE), F32)],
        input_output_aliases=aliases,
        compiler_params=_params("parallel", "arbitrary"),
        name="ssd",
    )(*args)


def _kv_newest_kernel(x_ref, wt_ref, *refs):
    k_ref, v_ref = refs[-2:]
    d = k_ref.shape[2]
    kv_t = lax.dot_general(wt_ref[...], x_ref[...].astype(BF16), (((1,), (1,)), ((), ())),
                           preferred_element_type=F32)
    k_ref[0, 0] = kv_t[:d]
    v_ref[0, 0] = kv_t[d:]


def _kv_newest(x, bsz, w_kv_t, layer, n_layers, buffers):
    rows, d_in = x.shape
    t = rows // bsz
    d = w_kv_t.shape[0] // 2
    assert t % WIN == 0
    blocks_per_seq = t // WIN
    in_specs = [pl.BlockSpec((WIN, d_in), lambda b: (b * blocks_per_seq + blocks_per_seq - 1, 0)),
                _const_spec((2 * d, d_in))]
    args = [x, w_kv_t]
    aliases = {}
    if buffers is not None:
        in_specs += [pl.BlockSpec(memory_space=pl.ANY)] * 2
        args += list(buffers)
        aliases = {2: 0, 3: 1}
    out_spec = pl.BlockSpec((1, 1, d, WIN), lambda b: (layer, b, 0, 0))
    out_shape = jax.ShapeDtypeStruct((n_layers, bsz, d, WIN), F32)
    return pl.pallas_call(
        _kv_newest_kernel,
        grid=(bsz,),
        in_specs=in_specs,
        out_specs=[out_spec, out_spec],
        out_shape=[out_shape, out_shape],
        input_output_aliases=aliases,
        compiler_params=_params("parallel"),
        name="kv_newest",
    )(*args)


def _attention_mixer(x, bsz, w_qkv, q_scale, bias, past, layer):
    rows, d = x.shape
    t = rows // bsz
    qkv = _dense(x, w_qkv, q_scale, out_dtype=BF16)
    o = _attention(qkv.reshape(bsz, t, 3 * d), bias, past, layer)
    return o.reshape(rows, d)


def _ssm_mixer(x, bsz, conv_state, ssm_state, layer, n_layers, states, w_zdt, w_xbc, dt_bias_row, conv_w,
               conv_b, a_log, d_skip, norm_w):
    rows, d = x.shape
    t = rows // bsz
    z, dt = _gate_and_dt(x, w_zdt, dt_bias_row)
    xa, new_conv = _xbc_conv(x.reshape(bsz, t, d), w_xbc, conv_state, conv_w, conv_b)
    d_inner = z.shape[1]
    y, states = _ssd(xa, z.reshape(bsz, t, d_inner), dt.reshape(bsz, t, LANES), ssm_state, a_log, d_skip,
                     norm_w, layer, n_layers, states)
    return y.reshape(rows, d_inner), new_conv, states


def kernel(x_prompt, x_sample, cache_k, cache_v, state_ssm, state_conv, w_qkv, rel_bias, w_attn_out,
           w_ssm_in, conv_w, conv_b, dt_bias, a_log, d_skip, ssm_norm_w, w_ssm_out, ln_mix_g, ln_mix_b,
           w_ff_up, w_ff_down, ln_ff_g, ln_ff_b):
    bp, tp, d = x_prompt.shape
    bs, ts, _ = x_sample.shape
    n_ssm_heads = dt_bias.shape[1]
    d_inner = ssm_norm_w.shape[1]
    conv_ch = conv_w.shape[2]
    xp = x_prompt.reshape(bp * tp, d)
    xs = x_sample.reshape(bs * ts, d)
    tq_p, tq_s = min(ATTN_Q_TILE, tp), min(ATTN_Q_TILE, ts)
    bias_p = _attn_bias(rel_bias, tq_p)
    bias_s = bias_p[:, :, :tq_s, :WIN + tq_s] if tq_s < tq_p else bias_p
    q_scale = jnp.concatenate([jnp.full((d,), HEAD_DIM ** -0.5 * LOG2E, F32), jnp.ones((2 * d,), F32)])
    n_att = w_qkv.shape[0]
    assert tp >= WIN and ts <= WIN
    past_k = cache_k.transpose(0, 1, 3, 4, 2).reshape(n_att, bs, d, WIN)
    past_v = cache_v.transpose(0, 1, 3, 4, 2).reshape(n_att, bs, d, WIN)
    kv_prompt = h_p = h_s = None
    n_ssm = w_ssm_in.shape[0]
    k_s, v_s, c_p, c_s = ([] for _ in range(4))
    for i in range(DEPTH):
        li = i // 2
        if i % 2 == 0:
            wq, wo = w_qkv[li].astype(BF16), w_attn_out[li].astype(BF16)
            kv_prompt = _kv_newest(xp, bp, wq[:, d:].T, li, n_att, kv_prompt)
            mp = _attention_mixer(xp, bp, wq, q_scale, bias_p[li], None, li)
            kv = _dense(xs, wq[:, d:], jnp.ones((2 * d,), F32), out_dtype=F32)
            k_s.append(kv[:, :d].reshape(bs, ts, N_HEADS, HEAD_DIM))
            v_s.append(kv[:, d:].reshape(bs, ts, N_HEADS, HEAD_DIM))
            ms = _attention_mixer(xs, bs, wq, q_scale, bias_s[li], (past_k, past_v), li)
        else:
            w_in = w_ssm_in[li].astype(BF16)
            w_zdt = jnp.concatenate([w_in[:, :d_inner],
                                     jnp.pad(w_in[:, d_inner + conv_ch:], ((0, 0), (0, LANES - n_ssm_heads)))],
                                    axis=1)
            w_xbc = w_in[:, d_inner:d_inner + conv_ch]
            dt_bias_row = jnp.pad(dt_bias[li], (0, LANES - n_ssm_heads)).reshape(1, LANES)
            args = (w_zdt, w_xbc, dt_bias_row, conv_w[li], conv_b[li], a_log[li], d_skip[li], ssm_norm_w[li])
            wo = w_ssm_out[li].astype(BF16)
            conv0 = jnp.zeros((bp, CONV_WIDTH - 1, conv_ch), F32)
            mp, cv, h_p = _ssm_mixer(xp, bp, conv0, None, li, n_ssm, h_p, *args)
            c_p.append(cv)
            ms, cv, h_s = _ssm_mixer(xs, bs, state_conv[li], state_ssm, li, n_ssm, h_s, *args)
            c_s.append(cv)
        tail = (ln_mix_g[i], ln_mix_b[i], w_ff_up[i].astype(BF16), w_ff_down[i].astype(BF16),
                ln_ff_g[i], ln_ff_b[i])
        xp = _block_tail(mp, wo, xp, *tail)
        xs = _block_tail(ms, wo, xs, *tail)
    k_p, v_p = (buf.reshape(n_att, bp, N_HEADS, HEAD_DIM, WIN).transpose(0, 1, 4, 2, 3) for buf in kv_prompt)
    state_shape = (n_ssm, -1, n_ssm_heads, SSM_HEAD_DIM, D_STATE)
    return (xp.reshape(bp, tp, d), xs.reshape(bs, ts, d), k_p, v_p, h_p.reshape(state_shape),
            jnp.stack(c_p), jnp.stack(k_s), jnp.stack(v_s), h_s.reshape(state_shape), jnp.stack(c_s))
```

```python
import functools
import math

import jax
import jax.numpy as jnp
from jax import lax
from jax.experimental import pallas as pl
from jax.experimental.pallas import tpu as pltpu

F32 = jnp.float32
BF16 = jnp.bfloat16

DEPTH = 4
CHUNK = 64
WIN = 8 * CHUNK
N_HEADS = 16
HEAD_DIM = 64
REL_CLIP = 256
SSM_HEAD_DIM = 64
SSM_GROUPS = 8
D_STATE = 128
CONV_WIDTH = 4
ALPHA = (2 * DEPTH) ** 0.25
LN_EPS = 1e-5
RMS_EPS = 1e-5
NEG_INF = -1e30
LOG2E = math.log2(math.e)

LANES = 128
SUBLANES = 8
VMEM_BYTES_V7X = 64 * 1024 * 1024
VMEM_LIMIT_BYTES = VMEM_BYTES_V7X - 8 * 1024 * 1024

ROW_TILE = 512
ATTN_Q_TILE = 128
ATTN_GROUP = 4
ATTN_STEP_ROWS = 512
XBC_COL_TILE = 4096
CONV_LANE_CHUNK = 256
MLP_FF_TILE = 2048
SSD_STAGE_GROUPS = 8


def _params(*semantics):
    return pltpu.CompilerParams(dimension_semantics=semantics, vmem_limit_bytes=VMEM_LIMIT_BYTES)


def _const_spec(shape):
    return pl.BlockSpec(shape, lambda *_: (0,) * len(shape), pipeline_mode=pl.Buffered(1))


def _layer_norm(y, g, b):
    mu = jnp.mean(y, axis=-1, keepdims=True)
    d = y - mu
    var = jnp.mean(d * d, axis=-1, keepdims=True)
    return d * lax.rsqrt(var + LN_EPS) * g + b


def _softplus(x):
    return jnp.maximum(x, 0.0) + jnp.log(1.0 + jnp.exp(-jnp.abs(x)))


def _silu(x):
    h = 0.5 * x
    return h + h * jnp.tanh(h)


def _row_tile(rows):
    tm = min(ROW_TILE, rows)
    assert rows % tm == 0
    return tm


def _dense_kernel(x_ref, w_ref, cs_ref, o_ref):
    acc = jnp.dot(x_ref[...].astype(BF16), w_ref[...], preferred_element_type=F32)
    o_ref[...] = (acc * cs_ref[...]).astype(o_ref.dtype)


def _dense(x, w, col_scale, *, out_dtype, row_blocks=None):
    rows, k = x.shape
    n = w.shape[1]
    tm = _row_tile(rows)
    n_rb, rb_fn = row_blocks if row_blocks is not None else (rows // tm, lambda i: i)
    return pl.pallas_call(
        _dense_kernel,
        grid=(n_rb,),
        in_specs=[pl.BlockSpec((tm, k), lambda i: (rb_fn(i), 0)),
                  _const_spec((k, n)),
                  _const_spec((1, n))],
        out_specs=pl.BlockSpec((tm, n), lambda i: (i, 0)),
        out_shape=jax.ShapeDtypeStruct((n_rb * tm, n), out_dtype),
        compiler_params=_params("parallel"),
        name="dense",
    )(x, w, col_scale.reshape(1, n).astype(F32))


def _block_tail_kernel(a_ref, wo_ref, x_ref, g1_ref, b1_ref, wu_ref, wd_ref, g2_ref, b2_ref, o_ref,
                       mix_ref, x1_ref, acc_ref):
    @pl.when(pl.program_id(0) == 0)
    def _():
        mix_ref[...] = jnp.zeros_like(mix_ref)
        x1_ref[...] = jnp.zeros_like(x1_ref)
        acc_ref[...] = jnp.zeros_like(acc_ref)

    out = _layer_norm(ALPHA * x1_ref[...] + acc_ref[...], g2_ref[...], b2_ref[...])
    mix = jnp.dot(a_ref[...], wo_ref[...], preferred_element_type=F32)
    x1 = _layer_norm(ALPHA * x_ref[...] + mix_ref[...], g1_ref[...], b1_ref[...])
    x1b = x1.astype(BF16)
    d_ff = wu_ref.shape[1]
    tf = min(MLP_FF_TILE, d_ff)
    acc = None
    for f0 in range(0, d_ff, tf):
        h = jnp.dot(x1b, wu_ref[:, f0:f0 + tf], preferred_element_type=F32)
        h = jnp.square(jnp.maximum(h, 0.0)).astype(BF16)
        part = jnp.dot(h, wd_ref[f0:f0 + tf, :], preferred_element_type=F32)
        acc = part if acc is None else acc + part
    o_ref[...] = out
    mix_ref[...] = mix
    x1_ref[...] = x1
    acc_ref[...] = acc


def _block_tail(a, w_o, x, g1, b1, w_up, w_down, g2, b2):
    rows, k = a.shape
    d = x.shape[1]
    d_ff = w_up.shape[1]
    tm = _row_tile(rows)
    n_tiles = rows // tm
    last = n_tiles - 1
    vec = lambda v: v.reshape(1, d).astype(F32)
    return pl.pallas_call(
        _block_tail_kernel,
        grid=(n_tiles + 2,),
        in_specs=[pl.BlockSpec((tm, k), lambda i: (jnp.minimum(i, last), 0)),
                  _const_spec((k, d)),
                  pl.BlockSpec((tm, d), lambda i: (jnp.clip(i - 1, 0, last), 0)),
                  _const_spec((1, d)), _const_spec((1, d)),
                  _const_spec((d, d_ff)),
                  _const_spec((d_ff, d)),
                  _const_spec((1, d)), _const_spec((1, d))],
        out_specs=pl.BlockSpec((tm, d), lambda i: (jnp.maximum(i - 2, 0), 0)),
        out_shape=jax.ShapeDtypeStruct((rows, d), F32),
        scratch_shapes=[pltpu.VMEM((tm, d), F32), pltpu.VMEM((tm, d), F32), pltpu.VMEM((tm, d), F32)],
        compiler_params=_params("arbitrary"),
        name="block_tail",
    )(a, w_o, x, vec(g1), vec(b1), w_up, w_down, vec(g2), vec(b2))


def _attn_kernel(*refs, tq, n_past, has_past):
    if has_past:
        q_ref, k_ref, v_ref, kp_ref, vp_ref, bias_ref, o_ref, kpad, vpad = refs
    else:
        q_ref, k_ref, v_ref, bias_ref, o_ref, kpad, vpad = refs
    n_seq, t = q_ref.shape[0], q_ref.shape[1]
    band = WIN + tq
    first_key = WIN - n_past
    assert first_key % LANES == 0
    for b in range(n_seq):
        if has_past:
            kpad[b, :, 0:WIN] = kp_ref[0, b].astype(BF16)
            vpad[b, 0:WIN, :] = vp_ref[0, b].T.astype(BF16)
        kpad[b, :, WIN:WIN + t] = k_ref[b].T
        vpad[b, WIN:WIN + t, :] = v_ref[b]

    first = lax.broadcasted_iota(jnp.int32, (tq, LANES), 1) < HEAD_DIM
    mask_a = first.astype(BF16)
    mask_b = jnp.logical_not(first).astype(BF16)

    units = [(b, qs) for b in range(n_seq) for qs in range(t // tq)]
    group = min(ATTN_GROUP, len(units))
    assert len(units) % group == 0

    def key_range(qs):
        return max(qs * tq, first_key), qs * tq + band

    def scores(g):
        out = []
        for b, qs in units[g * group:(g + 1) * group]:
            r0 = qs * tq
            lo, hi = key_range(qs)
            q2 = q_ref[b, r0:r0 + tq, :]
            qq = jnp.concatenate([q2 * mask_a, q2 * mask_b], axis=0)
            out.append(jnp.dot(qq, kpad[b, :, lo:hi], preferred_element_type=F32))
        return out

    def softmax_and_values(g, group_scores):
        probs = []
        for (b, qs), s in zip(units[g * group:(g + 1) * group], group_scores):
            lo, hi = key_range(qs)
            s = s + bias_ref[:, band - (hi - lo):]
            p = jnp.exp2(s - jnp.max(s, axis=-1, keepdims=True))
            probs.append((p.astype(BF16), jnp.sum(p, axis=-1, keepdims=True)))
        for (b, qs), (p, denom) in zip(units[g * group:(g + 1) * group], probs):
            r0 = qs * tq
            lo, hi = key_range(qs)
            o = jnp.dot(p, vpad[b, lo:hi, :], preferred_element_type=F32) / denom
            o_ref[b, r0:r0 + tq, :] = jnp.where(first, o[:tq], o[tq:]).astype(o_ref.dtype)

    n_groups = len(units) // group
    pending = scores(0)
    for g in range(n_groups):
        upcoming = scores(g + 1) if g + 1 < n_groups else None
        softmax_and_values(g, pending)
        pending = upcoming


def _attn_bias(rel_tables, tq):
    band = WIN + tq
    lo = REL_CLIP + 1 - tq
    assert lo >= 0
    n_diag = tq + band - 1
    n_flat = n_diag - (2 * REL_CLIP + 1 - lo)
    tab = rel_tables.astype(F32)
    by_dist = jnp.concatenate([tab[..., lo:], jnp.repeat(tab[..., -1:], n_flat, axis=-1)], axis=-1)
    far_first = jnp.concatenate([by_dist[..., ::-1], by_dist[..., :1]], axis=-1)
    lead = far_first.shape[:-1]
    skew = jnp.tile(far_first, (1,) * len(lead) + (tq,))[..., :tq * n_diag].reshape(lead + (tq, n_diag))
    bias = skew[..., tq - 1:tq - 1 + band]
    r = jnp.arange(tq)[:, None]
    j = jnp.arange(band)[None, :]
    q_chunk = r // CHUNK
    k_chunk = (j - WIN) // CHUNK
    in_band = (k_chunk >= q_chunk - WIN // CHUNK) & (k_chunk <= q_chunk)
    return jnp.where(in_band, bias * LOG2E, NEG_INF)


def _attention(qkv, bias, past=None, layer=0):
    bsz, t, d3 = qkv.shape
    d = d3 // 3
    n_pairs = d // LANES
    tq = bias.shape[1]
    band = WIN + tq
    assert t % tq == 0 and tq % CHUNK == 0 and bias.shape[2] == band
    has_past = past is not None
    n_seq = max(1, min(bsz, ATTN_STEP_ROWS // t))
    assert bsz % n_seq == 0
    new_spec = lambda off: pl.BlockSpec((n_seq, t, LANES), lambda hp, b: (b, 0, off * n_pairs + hp))
    in_specs = [new_spec(0), new_spec(1), new_spec(2)]
    args = [qkv, qkv, qkv]
    if has_past:
        past_spec = pl.BlockSpec((1, n_seq, LANES, WIN), lambda hp, b: (layer, b, hp, 0))
        in_specs += [past_spec, past_spec]
        args += list(past)
    in_specs.append(pl.BlockSpec((2 * tq, band), lambda hp, b: (hp, 0)))
    args.append(bias.reshape(N_HEADS * tq, band))
    return pl.pallas_call(
        functools.partial(_attn_kernel, tq=tq, n_past=WIN if has_past else 0, has_past=has_past),
        grid=(n_pairs, bsz // n_seq),
        in_specs=in_specs,
        out_specs=pl.BlockSpec((n_seq, t, LANES), lambda hp, b: (b, 0, hp)),
        out_shape=jax.ShapeDtypeStruct((bsz, t, d), BF16),
        scratch_shapes=[pltpu.VMEM((n_seq, LANES, WIN + t), BF16), pltpu.VMEM((n_seq, WIN + t, LANES), BF16)],
        compiler_params=_params("parallel", "parallel"),
        name="band_attention",
    )(*args)


def _zdt_kernel(x_ref, w_ref, bias_ref, z_ref, dt_ref):
    d_inner = z_ref.shape[1]
    acc = jnp.dot(x_ref[...].astype(BF16), w_ref[...], preferred_element_type=F32)
    z_ref[...] = _silu(acc[:, :d_inner]).astype(z_ref.dtype)
    dt_ref[...] = _softplus(acc[:, d_inner:] + bias_ref[...])


def _gate_and_dt(x, w_zdt, dt_bias_row):
    rows, k = x.shape
    n = w_zdt.shape[1]
    d_inner = n - LANES
    tm = _row_tile(rows)
    return pl.pallas_call(
        _zdt_kernel,
        grid=(rows // tm,),
        in_specs=[pl.BlockSpec((tm, k), lambda i: (i, 0)), _const_spec((k, n)), _const_spec((1, LANES))],
        out_specs=[pl.BlockSpec((tm, d_inner), lambda i: (i, 0)), pl.BlockSpec((tm, LANES), lambda i: (i, 0))],
        out_shape=[jax.ShapeDtypeStruct((rows, d_inner), BF16), jax.ShapeDtypeStruct((rows, LANES), F32)],
        compiler_params=_params("parallel"),
        name="gate_dt",
    )(x, w_zdt, dt_bias_row)


def _xbc_tile(s, nt, nc):
    return s // (nt * nc), (s // nc) % nt, s % nc


def _xbc_conv_kernel(x_ref, w_ref, st_ref, cw_ref, cb_ref, o_ref, tail_ref, xb_ref, halo_ref, xm_ref,
                     *, nt, nc):
    s = pl.program_id(0)
    n_tiles = pl.num_programs(0) - 1
    j_mm = jnp.minimum(s, n_tiles - 1) % nc
    _, i, j = _xbc_tile(jnp.maximum(s - 1, 0), nt, nc)
    tt = x_ref.shape[1]

    @pl.when(s == 0)
    def _():
        xm_ref[...] = jnp.zeros_like(xm_ref)

    @pl.when(j_mm == 0)
    def _():
        xb_ref[...] = x_ref[0].astype(BF16)

    @pl.when(i == 0)
    def _():
        halo_ref[j] = st_ref[0]

    tc = min(CONV_LANE_CHUNK, w_ref.shape[1])
    first_sublane = lax.broadcasted_iota(jnp.int32, (1, SUBLANES, tc), 1) == 0
    for c0 in range(0, w_ref.shape[1], tc):
        xm = xm_ref[:, c0:c0 + tc]
        xm_ref[:, c0:c0 + tc] = jnp.dot(xb_ref[...], w_ref[:, c0:c0 + tc], preferred_element_type=F32)
        halo = halo_ref[j, :, c0:c0 + tc]
        taps = [cw_ref[kk, :, c0:c0 + tc] for kk in range(CONV_WIDTH)]
        x3 = xm.reshape(tt // SUBLANES, SUBLANES, tc)
        acc = x3 * taps[CONV_WIDTH - 1]
        delayed = x3
        for m in range(1, CONV_WIDTH):
            rotated = pltpu.roll(delayed, 1, axis=1)
            before8 = jnp.broadcast_to(halo[SUBLANES - m:SUBLANES - m + 1, :], (1, SUBLANES, tc))
            from_prev_group = jnp.concatenate([before8, rotated[:-1]], axis=0)
            delayed = jnp.where(first_sublane, from_prev_group, rotated)
            acc = acc + delayed * taps[CONV_WIDTH - 1 - m]
        h = acc + cb_ref[:, c0:c0 + tc]
        o_ref[0, :, c0:c0 + tc] = (h + h * jnp.tanh(h)).reshape(tt, tc).astype(o_ref.dtype)
        tail = xm[tt - SUBLANES:, :]
        halo_ref[j, :, c0:c0 + tc] = tail
        tail_ref[0, 0, :, c0:c0 + tc] = tail


def _xbc_conv(x, w_xbc, conv_state, conv_w, conv_b):
    bsz, t, k = x.shape
    ch = w_xbc.shape[1]
    tt = min(ROW_TILE, t)
    tn = min(XBC_COL_TILE, ch)
    assert t % tt == 0 and ch % tn == 0 and tt >= SUBLANES
    state8 = jnp.pad(conv_state, ((0, 0), (SUBLANES - (CONV_WIDTH - 1), 0), (0, 0)))
    nt, nc = t // tt, ch // tn
    n_tiles = bsz * nt * nc
    proj = lambda s: _xbc_tile(jnp.minimum(s, n_tiles - 1), nt, nc)
    conv = lambda s: _xbc_tile(jnp.maximum(s - 1, 0), nt, nc)
    xa, tail = pl.pallas_call(
        functools.partial(_xbc_conv_kernel, nt=nt, nc=nc),
        grid=(n_tiles + 1,),
        in_specs=[pl.BlockSpec((1, tt, k), lambda s: (proj(s)[0], proj(s)[1], 0)),
                  pl.BlockSpec((k, tn), lambda s: (0, proj(s)[2])),
                  pl.BlockSpec((1, SUBLANES, tn), lambda s: (conv(s)[0], 0, conv(s)[2])),
                  pl.BlockSpec((CONV_WIDTH, SUBLANES, tn), lambda s: (0, 0, conv(s)[2])),
                  pl.BlockSpec((SUBLANES, tn), lambda s: (0, conv(s)[2]))],
        out_specs=[pl.BlockSpec((1, tt, tn), lambda s: conv(s)),
                   pl.BlockSpec((1, 1, SUBLANES, tn), lambda s: (conv(s)[0], conv(s)[1], 0, conv(s)[2]))],
        out_shape=[jax.ShapeDtypeStruct((bsz, t, ch), BF16),
                   jax.ShapeDtypeStruct((bsz, nt, SUBLANES, ch), F32)],
        scratch_shapes=[pltpu.VMEM((tt, k), BF16), pltpu.VMEM((nc, SUBLANES, tn), F32),
                        pltpu.VMEM((tt, tn), F32)],
        compiler_params=_params("arbitrary"),
        name="xbc_conv",
    )(x, w_xbc, state8,
      jnp.broadcast_to(0.5 * conv_w[:, None, :], (CONV_WIDTH, SUBLANES, ch)),
      jnp.broadcast_to(0.5 * conv_b[None, :], (SUBLANES, ch)))
    return xa, tail[:, -1, SUBLANES - (CONV_WIDTH - 1):, :]


def _dot_f32(a, b):
    return jnp.dot(a, b, preferred_element_type=F32, precision=lax.Precision.HIGHEST)


def _ssd_kernel(x_ref, bm_ref, cm_ref, z_ref, dtc_ref, dtr_ref, ac_ref, ar_ref, dsk_ref, nw_ref,
                *refs, has_h0):
    y_ref, hout_ref, h_ref = refs[-3:]
    c = pl.program_id(1)
    seq = x_ref.shape[1]
    n_pairs = x_ref.shape[2] // LANES
    pairs_per_group = n_pairs // SSM_GROUPS

    @pl.when(c == 0)
    def _():
        h_ref[...] = refs[0][0, 0] if has_h0 else jnp.zeros_like(h_ref)

    ri = lax.broadcasted_iota(jnp.int32, (seq, seq), 0)
    ci = lax.broadcasted_iota(jnp.int32, (seq, seq), 1)
    tri = (ri >= ci).astype(F32)

    cum_c = _dot_f32(tri, dtc_ref[0] * (-jnp.exp(ac_ref[...])))

    pi = lax.broadcasted_iota(jnp.int32, (LANES, LANES), 0)
    qi = lax.broadcasted_iota(jnp.int32, (LANES, LANES), 1)
    same_head = (pi < seq) == (qi < seq)
    tri_bd = (same_head & (pi <= qi)).astype(F32)
    ones_bd = same_head.astype(F32)
    dt_r = dtr_ref[0, 0]
    dta_r = dt_r * (-jnp.exp(ar_ref[...]))
    cum_r = _dot_f32(dta_r, tri_bd)
    last_r = _dot_f32(dta_r, ones_bd)
    w_in_r = jnp.exp(last_r - cum_r) * dt_r

    lane = lax.broadcasted_iota(jnp.int32, (seq, LANES), 1)
    first = lane < SSM_HEAD_DIM
    causal2 = jnp.where(first, lane, lane - seq) <= lax.broadcasted_iota(jnp.int32, (seq, LANES), 0)
    first_rows = lax.broadcasted_iota(jnp.int32, (LANES, LANES), 0) < SSM_HEAD_DIM
    e_last_c = jnp.exp(cum_c[seq - 1:seq, :])

    nt = (((1,), (1,)), ((), ()))

    def staged(groups):
        pairs = [j for g in groups for j in range(g * pairs_per_group, (g + 1) * pairs_per_group)]
        bg2 = {g: jnp.concatenate([bm_ref[0, :, g * D_STATE:(g + 1) * D_STATE]] * 2, axis=0) for g in groups}
        cgs = {g: cm_ref[0, :, g * D_STATE:(g + 1) * D_STATE] for g in groups}
        cb2 = {g: lax.dot_general(cgs[g], bg2[g], nt, preferred_element_type=F32) for g in groups}
        h_old = {j: h_ref[j * LANES:(j + 1) * LANES, :] for j in pairs}
        y_state = {j: lax.dot_general(cgs[j // pairs_per_group], h_old[j].astype(BF16), nt,
                                      preferred_element_type=F32) for j in pairs}

        xpf, col_cum, w2, x_bd, xw_t = {}, {}, {}, {}, {}
        for j in pairs:
            xf = x_ref[0, :, j * LANES:(j + 1) * LANES].astype(F32)
            cc = jnp.where(first, cum_c[:, 2 * j:2 * j + 1], cum_c[:, 2 * j + 1:2 * j + 2])
            decay = jnp.exp(jnp.where(causal2, cc - cum_r[j:j + 1, :], -jnp.inf))
            bd = jnp.concatenate([jnp.where(first, xf, 0.0), jnp.where(first, 0.0, xf)], axis=0)
            xpf[j] = xf
            col_cum[j] = cc
            w2[j] = (cb2[j // pairs_per_group] * decay * dt_r[j:j + 1, :]).astype(BF16)
            x_bd[j] = bd.astype(BF16)
            xw_t[j] = (bd.T * w_in_r[j:j + 1, :]).astype(BF16)

        y_intra = {j: jnp.dot(w2[j], x_bd[j], preferred_element_type=F32) for j in pairs}
        upd = {j: jnp.dot(xw_t[j], bg2[j // pairs_per_group], preferred_element_type=F32) for j in pairs}

        for j in pairs:
            e_last = jnp.where(first_rows, e_last_c[:, 2 * j:2 * j + 1], e_last_c[:, 2 * j + 1:2 * j + 2])
            h_ref[j * LANES:(j + 1) * LANES, :] = h_old[j] * e_last + upd[j]
        for g in groups:
            ys = [y_intra[j] + jnp.exp(col_cum[j]) * y_state[j] + dsk_ref[:, j * LANES:(j + 1) * LANES] * xpf[j]
                  for j in range(g * pairs_per_group, (g + 1) * pairs_per_group)]
            glo = g * pairs_per_group * LANES
            ghi = glo + pairs_per_group * LANES
            yg = jnp.concatenate(ys, axis=1) * z_ref[0, :, glo:ghi].astype(F32)
            ms = jnp.mean(yg * yg, axis=-1, keepdims=True)
            y_ref[0, :, glo:ghi] = (yg * lax.rsqrt(ms + RMS_EPS) * nw_ref[:, glo:ghi]).astype(y_ref.dtype)

    for g0 in range(0, SSM_GROUPS, SSD_STAGE_GROUPS):
        staged(list(range(g0, g0 + SSD_STAGE_GROUPS)))

    @pl.when(c == pl.num_programs(1) - 1)
    def _():
        hout_ref[0, 0] = h_ref[...]


def _ssd(xa, z, dt, h0, a_log, d_skip, norm_w, layer, n_layers, states):
    bsz, t, _ = xa.shape
    d_inner = z.shape[2]
    n_heads = d_inner // SSM_HEAD_DIM
    n_pairs = n_heads // 2
    gn = SSM_GROUPS * D_STATE
    assert t % CHUNK == 0 and d_inner % gn == 0
    nc = t // CHUNK
    dt_rows = dt[:, :, :n_heads].reshape(bsz, nc, CHUNK, n_pairs, 2)
    dt_rows = dt_rows.transpose(0, 1, 3, 4, 2).reshape(bsz, nc, n_pairs, 2 * CHUNK)
    a_col = jnp.pad(a_log, (0, LANES - n_heads)).reshape(1, LANES)
    a_rows = jnp.repeat(a_log, CHUNK).reshape(n_pairs, 2 * CHUNK)
    d_skip_row = jnp.repeat(d_skip, SSM_HEAD_DIM).reshape(1, d_inner)
    x_blocks = d_inner // gn
    state_spec = pl.BlockSpec((1, 1, d_inner, D_STATE), lambda b, c: (layer, b, 0, 0))
    in_specs = [pl.BlockSpec((1, CHUNK, d_inner), lambda b, c: (b, c, 0)),
                pl.BlockSpec((1, CHUNK, gn), lambda b, c: (b, c, x_blocks)),
                pl.BlockSpec((1, CHUNK, gn), lambda b, c: (b, c, x_blocks + 1)),
                pl.BlockSpec((1, CHUNK, d_inner), lambda b, c: (b, c, 0)),
                pl.BlockSpec((1, CHUNK, LANES), lambda b, c: (b, c, 0)),
                pl.BlockSpec((1, 1, n_pairs, LANES), lambda b, c: (b, c, 0, 0)),
                _const_spec((1, LANES)),
                _const_spec((n_pairs, LANES)),
                _const_spec((1, d_inner)),
                _const_spec((1, d_inner))]
    args = [xa, xa, xa, z, dt, dt_rows, a_col, a_rows, d_skip_row, norm_w.reshape(1, d_inner)]
    if h0 is not None:
        in_specs.append(state_spec)
        args.append(h0.reshape(n_layers, bsz, d_inner, D_STATE))
    aliases = {len(args): 1}
    in_specs.append(pl.BlockSpec(memory_space=pl.ANY))
    args.append(states)
    return pl.pallas_call(
        functools.partial(_ssd_kernel, has_h0=h0 is not None),
        grid=(bsz, nc),
        in_specs=in_specs,
        out_specs=[pl.BlockSpec((1, CHUNK, d_inner), lambda b, c: (b, c, 0)), state_spec],
        out_shape=[jax.ShapeDtypeStruct((bsz, t, d_inner), BF16),
                   jax.ShapeDtypeStruct((n_layers, bsz, d_inner, D_STATE), F32)],
        scratch_shapes=[pltpu.VMEM((d_inner, D_STATE), F32)],
        input_output_aliases=aliases,
        compiler_params=_params("parallel", "arbitrary"),
        name="ssd",
    )(*args)


def _kv_newest_kernel(x_ref, wt_ref, *refs):
    k_ref, v_ref = refs[-2:]
    d = k_ref.shape[2]
    kv_t = lax.dot_general(wt_ref[...], x_ref[...].astype(BF16), (((1,), (1,)), ((), ())),
                           preferred_element_type=F32)
    k_ref[0, 0] = kv_t[:d]
    v_ref[0, 0] = kv_t[d:]


def _kv_newest(x, bsz, w_kv_t, layer, n_layers, buffers):
    rows, d_in = x.shape
    t = rows // bsz
    d = w_kv_t.shape[0] // 2
    assert t % WIN == 0
    blocks_per_seq = t // WIN
    in_specs = [pl.BlockSpec((WIN, d_in), lambda b: (b * blocks_per_seq + blocks_per_seq - 1, 0)),
                _const_spec((2 * d, d_in))]
    in_specs += [pl.BlockSpec(memory_space=pl.ANY)] * 2
    out_spec = pl.BlockSpec((1, 1, d, WIN), lambda b: (layer, b, 0, 0))
    out_shape = jax.ShapeDtypeStruct((n_layers, bsz, d, WIN), F32)
    return pl.pallas_call(
        _kv_newest_kernel,
        grid=(bsz,),
        in_specs=in_specs,
        out_specs=[out_spec, out_spec],
        out_shape=[out_shape, out_shape],
        input_output_aliases={2: 0, 3: 1},
        compiler_params=_params("parallel"),
        name="kv_newest",
    )(x, w_kv_t, *buffers)


def _attention_mixer(x, bsz, w_qkv, q_scale, bias, past, layer):
    rows, d = x.shape
    t = rows // bsz
    qkv = _dense(x, w_qkv, q_scale, out_dtype=BF16)
    o = _attention(qkv.reshape(bsz, t, 3 * d), bias, past, layer)
    return o.reshape(rows, d)


def _ssm_mixer(x, bsz, conv_state, ssm_state, layer, n_layers, states, w_zdt, w_xbc, dt_bias_row, conv_w,
               conv_b, a_log, d_skip, norm_w):
    rows, d = x.shape
    t = rows // bsz
    z, dt = _gate_and_dt(x, w_zdt, dt_bias_row)
    xa, new_conv = _xbc_conv(x.reshape(bsz, t, d), w_xbc, conv_state, conv_w, conv_b)
    d_inner = z.shape[1]
    y, states = _ssd(xa, z.reshape(bsz, t, d_inner), dt.reshape(bsz, t, LANES), ssm_state, a_log, d_skip,
                     norm_w, layer, n_layers, states)
    return y.reshape(rows, d_inner), new_conv, states


def kernel(x_prompt, x_sample, cache_k, cache_v, state_ssm, state_conv, w_qkv, rel_bias, w_attn_out,
           w_ssm_in, conv_w, conv_b, dt_bias, a_log, d_skip, ssm_norm_w, w_ssm_out, ln_mix_g, ln_mix_b,
           w_ff_up, w_ff_down, ln_ff_g, ln_ff_b):
    bp, tp, d = x_prompt.shape
    bs, ts, _ = x_sample.shape
    n_ssm_heads = dt_bias.shape[1]
    d_inner = ssm_norm_w.shape[1]
    conv_ch = conv_w.shape[2]
    xp = x_prompt.reshape(bp * tp, d)
    xs = x_sample.reshape(bs * ts, d)
    tq_p, tq_s = min(ATTN_Q_TILE, tp), min(ATTN_Q_TILE, ts)
    bias_p = _attn_bias(rel_bias, tq_p)
    bias_s = bias_p[:, :, :tq_s, :WIN + tq_s] if tq_s < tq_p else bias_p
    q_scale = jnp.concatenate([jnp.full((d,), HEAD_DIM ** -0.5 * LOG2E, F32), jnp.ones((2 * d,), F32)])
    n_att = w_qkv.shape[0]
    assert tp >= WIN and ts <= WIN
    past_k = cache_k.transpose(0, 1, 3, 4, 2).reshape(n_att, bs, d, WIN)
    past_v = cache_v.transpose(0, 1, 3, 4, 2).reshape(n_att, bs, d, WIN)
    n_ssm = w_ssm_in.shape[0]
    kv_prompt = (jnp.zeros((n_att, bp, d, WIN), F32), jnp.zeros((n_att, bp, d, WIN), F32))
    h_p = jnp.zeros((n_ssm, bp, d_inner, D_STATE), F32)
    h_s = jnp.zeros((n_ssm, bs, d_inner, D_STATE), F32)
    k_s, v_s, c_p, c_s = ([] for _ in range(4))
    for i in range(DEPTH):
        li = i // 2
        if i % 2 == 0:
            wq, wo = w_qkv[li].astype(BF16), w_attn_out[li].astype(BF16)
            kv_prompt = _kv_newest(xp, bp, wq[:, d:].T, li, n_att, kv_prompt)
            mp = _attention_mixer(xp, bp, wq, q_scale, bias_p[li], None, li)
            kv = _dense(xs, wq[:, d:], jnp.ones((2 * d,), F32), out_dtype=F32)
            k_s.append(kv[:, :d].reshape(bs, ts, N_HEADS, HEAD_DIM))
            v_s.append(kv[:, d:].reshape(bs, ts, N_HEADS, HEAD_DIM))
            ms = _attention_mixer(xs, bs, wq, q_scale, bias_s[li], (past_k, past_v), li)
        else:
            w_in = w_ssm_in[li].astype(BF16)
            w_zdt = jnp.concatenate([w_in[:, :d_inner],
                                     jnp.pad(w_in[:, d_inner + conv_ch:], ((0, 0), (0, LANES - n_ssm_heads)))],
                                    axis=1)
            w_xbc = w_in[:, d_inner:d_inner + conv_ch]
            dt_bias_row = jnp.pad(dt_bias[li], (0, LANES - n_ssm_heads)).reshape(1, LANES)
            args = (w_zdt, w_xbc, dt_bias_row, conv_w[li], conv_b[li], a_log[li], d_skip[li], ssm_norm_w[li])
            wo = w_ssm_out[li].astype(BF16)
            conv0 = jnp.zeros((bp, CONV_WIDTH - 1, conv_ch), F32)
            mp, cv, h_p = _ssm_mixer(xp, bp, conv0, None, li, n_ssm, h_p, *args)
            c_p.append(cv)
            ms, cv, h_s = _ssm_mixer(xs, bs, state_conv[li], state_ssm, li, n_ssm, h_s, *args)
            c_s.append(cv)
        tail = (ln_mix_g[i], ln_mix_b[i], w_ff_up[i].astype(BF16), w_ff_down[i].astype(BF16),
                ln_ff_g[i], ln_ff_b[i])
        xp = _block_tail(mp, wo, xp, *tail)
        xs = _block_tail(ms, wo, xs, *tail)
    k_p, v_p = (buf.reshape(n_att, bp, N_HEADS, HEAD_DIM, WIN).transpose(0, 1, 4, 2, 3) for buf in kv_prompt)
    state_shape = (n_ssm, -1, n_ssm_heads, SSM_HEAD_DIM, D_STATE)
    return (xp.reshape(bp, tp, d), xs.reshape(bs, ts, d), k_p, v_p, h_p.reshape(state_shape),
            jnp.stack(c_p), jnp.stack(k_s), jnp.stack(v_s), h_s.reshape(state_shape), jnp.stack(c_s))
```
